```python
import math
import jax, jax.numpy as jnp
from jax import lax
import numpy as np

D_MODEL = 2048
BATCH = 8
SEQ = 2048
DEPTH = 1

ATTN_HEADS = 16
HEAD_DIM = 128
ATTN_W = ATTN_HEADS * HEAD_DIM
DILATED_GROUPS = ((128, 1), (512, 4), (2048, 16))
BLK = 128
LRU_W = D_MODEL
LRU_HEADS = 16
LRU_BLK = LRU_W // LRU_HEADS
CONV_W = 4
LRU_C = 8.0
D_FF = 4 * D_MODEL
EPS = 1e-6
IN_SPLITS = (ATTN_W, ATTN_W, ATTN_W, LRU_W, LRU_W, D_MODEL, D_MODEL)
IN_COLS = sum(IN_SPLITS)

kernel_name = "hybrid_dilated_attn_rglru_gated_block"


def rms_norm(x, g):
    xf = x.astype(jnp.float32)
    y = xf * lax.rsqrt(jnp.mean(xf * xf, axis=-1, keepdims=True) + EPS)
    return (y * g.astype(jnp.float32)).astype(x.dtype)


def alibi_slopes(n_heads):
    return 2.0 ** (-8.0 * jnp.arange(1, n_heads + 1, dtype=jnp.float32) / n_heads)


def dilated_group_attention(q, k, v, slopes, window, dilation):
    b, s, h, dh = q.shape
    d = dilation
    span = window // d
    u_len = s // d
    n_blk = -(-u_len // BLK)
    u_pad = n_blk * BLK
    qs = q.reshape(b, u_len, d, h, dh)
    ks = k.reshape(b, u_len, d, h, dh)
    vs = v.reshape(b, u_len, d, h, dh)
    qb = jnp.pad(qs, ((0, 0), (0, u_pad - u_len), (0, 0), (0, 0), (0, 0))).reshape(b, n_blk, BLK, d, h, dh)
    pad_kv = ((0, 0), (BLK, u_pad - u_len), (0, 0), (0, 0), (0, 0))
    kp = jnp.pad(ks, pad_kv).reshape(b, n_blk + 1, BLK, d, h, dh)
    vp = jnp.pad(vs, pad_kv).reshape(b, n_blk + 1, BLK, d, h, dh)
    kwin = jnp.concatenate([kp[:, :-1], kp[:, 1:]], axis=2)
    vwin = jnp.concatenate([vp[:, :-1], vp[:, 1:]], axis=2)
    scores = jnp.einsum('bnqrhe,bnkrhe->bnrhqk', qb, kwin).astype(jnp.float32) * (HEAD_DIM ** -0.5)
    i = jnp.arange(BLK)[:, None]
    j = jnp.arange(2 * BLK)[None, :]
    diff = BLK + i - j
    band = (diff >= 0) & (diff <= span)
    n_idx = jnp.arange(n_blk)[:, None, None]
    valid = band[None] & ((n_idx - 1) * BLK + j[None] >= 0)
    bias = -(slopes * d)[:, None, None] * diff.astype(jnp.float32)[None]
    scores = scores + bias[None, None, None]
    scores = jnp.where(valid[None, :, None, None], scores, -jnp.inf)
    m = jnp.max(scores, axis=-1, keepdims=True)
    p = jnp.exp(scores - m)
    l = jnp.sum(p, axis=-1)
    o = jnp.einsum('bnrhqk,bnkrhe->bnqrhe', p.astype(v.dtype), vwin).astype(jnp.float32)
    l_t = jnp.transpose(l, (0, 1, 4, 2, 3))
    o = o / l_t[..., None]
    lse = jnp.transpose(m[..., 0] + jnp.log(l), (0, 1, 4, 2, 3))
    o = o.reshape(b, u_pad, d, h, dh)[:, :u_len].reshape(b, s, h, dh)
    lse = lse.reshape(b, u_pad, d, h)[:, :u_len].reshape(b, s, h)
    return o, lse


def dilated_attention(q, k, v):
    b, s, _ = q.shape
    q = q.reshape(b, s, ATTN_HEADS, HEAD_DIM)
    k = k.reshape(b, s, ATTN_HEADS, HEAD_DIM)
    v = v.reshape(b, s, ATTN_HEADS, HEAD_DIM)
    slopes = alibi_slopes(ATTN_HEADS)
    outs, lses = [], []
    for window, dilation in DILATED_GROUPS:
        o, lse = dilated_group_attention(q, k, v, slopes, window, dilation)
        outs.append(o)
        lses.append(lse)
    w = jax.nn.softmax(jnp.stack(lses, axis=0), axis=0)
    out = jnp.sum(w[..., None] * jnp.stack(outs, axis=0), axis=0)
    return out.reshape(b, s, ATTN_W).astype(q.dtype)


def rg_lru_branch(xr, gate, conv_w, conv_b, wa, ba, wx, bx, lam):
    b, s, c = xr.shape
    xc = lax.conv_general_dilated(
        xr, conv_w.reshape(CONV_W, 1, c).astype(xr.dtype), window_strides=(1,),
        padding=[(CONV_W - 1, 0)], dimension_numbers=('NWC', 'WIO', 'NWC'),
        feature_group_count=c) + conv_b
    xh = xc.reshape(b, s, LRU_HEADS, LRU_BLK)
    r = jax.nn.sigmoid((jnp.einsum('bshi,hij->bshj', xh, wa).reshape(b, s, c) + ba).astype(jnp.float32))
    ig = jax.nn.sigmoid((jnp.einsum('bshi,hij->bshj', xh, wx).reshape(b, s, c) + bx).astype(jnp.float32))
    log_a = -LRU_C * r * jax.nn.softplus(-lam.astype(jnp.float32))
    a = jnp.exp(log_a)
    mult = jnp.sqrt(-jnp.expm1(2.0 * log_a))
    u = mult * (ig * xc.astype(jnp.float32))

    def combine(c1, c2):
        a1, b1 = c1
        a2, b2 = c2
        return a1 * a2, a2 * b1 + b2

    _, hseq = lax.associative_scan(combine, (a, u), axis=1)
    return (hseq * jax.nn.gelu(gate.astype(jnp.float32))).astype(xr.dtype)


def setup_inputs(seed: int = 0) -> dict:
    key = jax.random.key(seed)
    ks = jax.random.split(key, 20)
    f32 = jnp.float32
    nrm = lambda k, shape, scale: jax.random.normal(k, shape, f32) * scale
    a8 = jax.random.uniform(ks[9], (DEPTH, LRU_W), f32, 0.9, 0.999)
    a_base = a8 ** (1.0 / LRU_C)
    lru_lambda = jnp.log(a_base) - jnp.log1p(-a_base)
    return {
        "x": jax.random.normal(ks[0], (BATCH, SEQ, D_MODEL), f32),
        "norm_mix_g": 1.0 + nrm(ks[1], (DEPTH, D_MODEL), 0.02),
        "w_in": nrm(ks[2], (DEPTH, D_MODEL, IN_COLS), D_MODEL ** -0.5),
        "conv_w": nrm(ks[3], (DEPTH, CONV_W, LRU_W), CONV_W ** -0.5),
        "conv_b": nrm(ks[4], (DEPTH, LRU_W), 0.02),
        "lru_wa": nrm(ks[5], (DEPTH, LRU_HEADS, LRU_BLK, LRU_BLK), LRU_BLK ** -0.5),
        "lru_ba": nrm(ks[6], (DEPTH, LRU_W), 0.02),
        "lru_wx": nrm(ks[7], (DEPTH, LRU_HEADS, LRU_BLK, LRU_BLK), LRU_BLK ** -0.5),
        "lru_bx": nrm(ks[8], (DEPTH, LRU_W), 0.02),
        "lru_lambda": lru_lambda,
        "w_proj_attn": nrm(ks[10], (DEPTH, ATTN_W, D_MODEL), ATTN_W ** -0.5),
        "w_proj_lru": nrm(ks[11], (DEPTH, LRU_W, D_MODEL), LRU_W ** -0.5),
        "w_out": nrm(ks[12], (DEPTH, D_MODEL, D_MODEL), D_MODEL ** -0.5),
        "norm_mlp_g": 1.0 + nrm(ks[13], (DEPTH, D_MODEL), 0.02),
        "w_up": nrm(ks[14], (DEPTH, D_MODEL, D_FF), D_MODEL ** -0.5),
        "w_down": nrm(ks[15], (DEPTH, D_FF, D_MODEL), D_FF ** -0.5),
        "norm_final_g": 1.0 + nrm(ks[16], (D_MODEL,), 0.02),
    }


def reference(x, norm_mix_g, w_in, conv_w, conv_b, lru_wa, lru_ba, lru_wx, lru_bx, lru_lambda,
              w_proj_attn, w_proj_lru, w_out, norm_mlp_g, w_up, w_down, norm_final_g):
    h = x
    split_idx = list(np.cumsum(IN_SPLITS)[:-1])
    for l in range(DEPTH):
        xn = rms_norm(h, norm_mix_g[l])
        proj = jnp.einsum('bsd,dc->bsc', xn, w_in[l])
        q, k, v, xr, xg, g_attn, g_lru = jnp.split(proj, split_idx, axis=-1)
        y_attn = dilated_attention(q, k, v)
        y_lru = rg_lru_branch(xr, xg, conv_w[l], conv_b[l], lru_wa[l], lru_ba[l],
                              lru_wx[l], lru_bx[l], lru_lambda[l])
        merged = (jax.nn.sigmoid(g_attn) * jnp.einsum('bsc,cd->bsd', y_attn, w_proj_attn[l])
                  + jax.nn.sigmoid(g_lru) * jnp.einsum('bsc,cd->bsd', y_lru, w_proj_lru[l]))
        h = h + jnp.einsum('bsd,de->bse', merged, w_out[l])
        hn = rms_norm(h, norm_mlp_g[l])
        hid = jnp.square(jax.nn.relu(jnp.einsum('bsd,df->bsf', hn, w_up[l])))
        h = h + jnp.einsum('bsf,fd->bsd', hid, w_down[l])
    return rms_norm(h, norm_final_g)
```

```python
import functools

import jax
import jax.numpy as jnp
import numpy as np
from jax import lax
from jax.experimental import pallas as pl
from jax.experimental.pallas import tpu as pltpu

D_MODEL = 2048
SEQ = 2048
HEADS = 16
HEAD_DIM = 128
D_FF = 4 * D_MODEL
IN_COLS = 7 * D_MODEL
CONV_W = 4
LRU_C = 8.0
EPS = 1e-6

STRIDE = 16
SLAB = SEQ // STRIDE
NEG_BIG = 1e30

V7X_VMEM_BYTES = 64 * 1024 * 1024
VMEM_LIMIT = V7X_VMEM_BYTES - 8 * 1024 * 1024

BF16 = jnp.bfloat16
F32 = jnp.float32


def _cparams(*sem):
    return pltpu.CompilerParams(dimension_semantics=sem, vmem_limit_bytes=VMEM_LIMIT)


IN_TM = 1024
IN_TN = 1024


def _in_proj_kernel(x_ref, g_ref, w_ref, o_ref, xn_ref):
    @pl.when(pl.program_id(2) == 0)
    def _():
        g = g_ref[...]
        for s in range(IN_TM // SLAB):
            xs = x_ref[:, s * D_MODEL:(s + 1) * D_MODEL]
            ms = jnp.mean(xs * xs, axis=-1, keepdims=True)
            xn_ref[s * SLAB:(s + 1) * SLAB, :] = (xs * lax.rsqrt(ms + EPS) * g).astype(BF16)

    o_ref[...] = jnp.dot(xn_ref[...], w_ref[...], preferred_element_type=F32).astype(o_ref.dtype)


def _in_proj(x, g, w_bf16):
    b = x.shape[0]
    halves = STRIDE * SLAB // IN_TM
    xv = x.reshape(b, SLAB, STRIDE * D_MODEL)
    return pl.pallas_call(
        _in_proj_kernel,
        grid=(b, halves, IN_COLS // IN_TN),
        in_specs=[
            pl.BlockSpec((None, SLAB, (IN_TM // SLAB) * D_MODEL), lambda bi, i, j: (bi, 0, i)),
            pl.BlockSpec((1, D_MODEL), lambda bi, i, j: (0, 0)),
            pl.BlockSpec((D_MODEL, IN_TN), lambda bi, i, j: (0, j)),
        ],
        out_specs=pl.BlockSpec((IN_TM, IN_TN), lambda bi, i, j: (bi * halves + i, j)),
        out_shape=jax.ShapeDtypeStruct((b * SEQ, IN_COLS), BF16),
        scratch_shapes=[pltpu.VMEM((IN_TM, D_MODEL), BF16)],
        compiler_params=_cparams("parallel", "parallel", "arbitrary"),
        name="in_proj",
    )(xv, g.reshape(1, D_MODEL), w_bf16)


def _distance_tables():
    def table(tq, tk, span):
        dist = tq[:, None] - tk[None, :]
        ok = (dist >= 0) & (dist <= span) & (tk[None, :] >= 0)
        return np.where(ok, dist, NEG_BIG).astype(np.float32)

    u = np.arange(SLAB)
    d16 = table(16 * u, 16 * u, 128 * 16)
    m4, ul = np.meshgrid(np.arange(4), np.arange(32), indexing="ij")
    tq4 = (16 * (32 + ul) + 4 * m4).reshape(-1)
    m4k, ulk = np.meshgrid(np.arange(4), np.arange(64), indexing="ij")
    tk4 = (16 * ulk + 4 * m4k).reshape(-1)
    d4 = table(tq4, tk4, 128 * 4)
    tk4a = (16 * (32 + ul) + 4 * m4).reshape(-1)
    d4a = table(tq4, tk4a, 128 * 4)
    r1, ul1 = np.meshgrid(np.arange(16), np.arange(16), indexing="ij")
    tq1 = (16 * (16 + ul1) + r1).reshape(-1)
    r1k, ul1k = np.meshgrid(np.arange(16), np.arange(32), indexing="ij")
    tk1 = (16 * ul1k + r1k).reshape(-1)
    d1 = table(tq1, tk1, 128)
    d1a = table(tq1, tq1, 128)
    return d16, d4a, d4, d1a, d1


def _rows(ref, starts, size):
    parts = [ref[s:s + size, :] for s in starts]
    return parts[0] if len(parts) == 1 else jnp.concatenate(parts, axis=0)


def _attn_kernel(q_ref, k_ref, v_ref, ns_ref, d16_ref, d4a_ref, d4_ref, d1a_ref, d1_ref,
                 o_ref, acc_ref, m_ref, l_ref):
    scale = HEAD_DIM ** -0.5

    def scores(q, k, dist_ref):
        nk = dist_ref.shape[1]
        s = lax.dot_general(q, k, (((1,), (1,)), ((), ())), preferred_element_type=F32)
        return s * scale + ns_ref[0:1, 0:nk] * dist_ref[...]

    for r in range(STRIDE):
        rows = slice(r * SLAB, (r + 1) * SLAB)
        s = scores(q_ref[rows, :], k_ref[rows, :], d16_ref)
        m = jnp.max(s, axis=-1, keepdims=True)
        p = jnp.exp(s - m)
        acc_ref[rows, :] = jnp.dot(p.astype(BF16), v_ref[rows, :], preferred_element_type=F32)
        m_ref[rows, :] = jnp.broadcast_to(m, (SLAB, HEAD_DIM))
        l_ref[rows, :] = jnp.broadcast_to(jnp.sum(p, axis=-1, keepdims=True), (SLAB, HEAD_DIM))

    def merge_block(q_starts, q_size, k_starts, k_size, dist_ref, final):
        q = _rows(q_ref, q_starts, q_size)
        k = _rows(k_ref, k_starts, k_size)
        v = _rows(v_ref, k_starts, k_size)
        s = scores(q, k, dist_ref)
        m_old = _rows(m_ref, q_starts, q_size)[:, 0:1]
        l_old = _rows(l_ref, q_starts, q_size)[:, 0:1]
        acc_old = _rows(acc_ref, q_starts, q_size)
        m_new = jnp.maximum(m_old, jnp.max(s, axis=-1, keepdims=True))
        alpha = jnp.exp(m_old - m_new)
        p = jnp.exp(s - m_new)
        l_new = alpha * l_old + jnp.sum(p, axis=-1, keepdims=True)
        acc = alpha * acc_old + jnp.dot(p.astype(BF16), v, preferred_element_type=F32)
        if final:
            out = (acc / l_new).astype(o_ref.dtype)
            for i, st in enumerate(q_starts):
                o_ref[st:st + q_size, :] = out[i * q_size:(i + 1) * q_size, :]
        else:
            for i, st in enumerate(q_starts):
                sl = slice(i * q_size, (i + 1) * q_size)
                acc_ref[st:st + q_size, :] = acc[sl, :]
                m_ref[st:st + q_size, :] = jnp.broadcast_to(m_new[sl, :], (q_size, HEAD_DIM))
                l_ref[st:st + q_size, :] = jnp.broadcast_to(l_new[sl, :], (q_size, HEAD_DIM))

    for r4 in range(4):
        slabs = [(r4 + 4 * m) * SLAB for m in range(4)]
        for n in range(4):
            q_starts = [b + 32 * n for b in slabs]
            if n == 0:
                merge_block(q_starts, 32, q_starts, 32, d4a_ref, False)
            else:
                merge_block(q_starts, 32, [b + 32 * (n - 1) for b in slabs], 64, d4_ref, False)

    for n in range(SLAB // 16):
        q_starts = [r * SLAB + 16 * n for r in range(STRIDE)]
        if n == 0:
            merge_block(q_starts, 16, q_starts, 16, d1a_ref, True)
        else:
            merge_block(q_starts, 16, [b - 16 for b in q_starts], 32, d1_ref, True)


def _attention(proj, batch):
    tables = [jnp.asarray(t) for t in _distance_tables()]
    slopes = 2.0 ** (-8.0 * np.arange(1, HEADS + 1, dtype=np.float64) / HEADS)
    neg_slopes = jnp.asarray(np.broadcast_to(-slopes[:, None, None], (HEADS, 8, 512)).astype(np.float32))
    const = lambda t: pl.BlockSpec(t.shape, lambda b, h: (0, 0))
    head_block = lambda off: pl.BlockSpec((SEQ, HEAD_DIM), lambda b, h: (b, off + h))
    return pl.pallas_call(
        _attn_kernel,
        grid=(batch, HEADS),
        in_specs=[head_block(0), head_block(HEADS), head_block(2 * HEADS),
                  pl.BlockSpec((None, 8, 512), lambda b, h: (h, 0, 0))] + [const(t) for t in tables],
        out_specs=pl.BlockSpec((SEQ, HEAD_DIM), lambda b, h: (b, h)),
        out_shape=jax.ShapeDtypeStruct((batch * SEQ, D_MODEL), BF16),
        scratch_shapes=[pltpu.VMEM((SEQ, HEAD_DIM), F32)] * 3,
        compiler_params=_cparams("parallel", "parallel"),
        name="dilated_attention",
    )(proj, proj, proj, neg_slopes, *tables)


def _shift_rows(x, k, fill):
    n = x.shape[0]
    if k % 8 == 0:
        return jnp.concatenate([jnp.full((k, x.shape[1]), fill, x.dtype), x[:n - k, :]], axis=0)
    rolled = pltpu.roll(x, k, axis=0)
    row = lax.broadcasted_iota(jnp.int32, x.shape, 0)
    return jnp.where(row < k, fill, rolled)


def _sigmoid(x):
    return 1.0 / (1.0 + jnp.exp(-x))


def _lru_kernel(xr_ref, xg_ref, cw_ref, cb_ref, wa_ref, ba_ref, wx_ref, bx_ref, lam_ref,
                o_ref, xc_ref, a_ref, b_ref):
    cw = cw_ref[...]
    cb = cb_ref[...]

    def tap(r):
        x = xr_ref[(r % STRIDE) * SLAB:(r % STRIDE + 1) * SLAB, :].astype(F32)
        return x if r >= 0 else _shift_rows(x, 1, 0.0)

    for r in range(STRIDE):
        xc = cb + cw[3:4, :] * tap(r)
        for j in range(1, CONV_W):
            xc = xc + cw[3 - j:4 - j, :] * tap(r - j)
        xc_ref[r * SLAB:(r + 1) * SLAB, :] = xc

    xc_bf = xc_ref[...].astype(BF16)
    rg = _sigmoid(jnp.dot(xc_bf, wa_ref[...], preferred_element_type=F32) + ba_ref[...])
    ig = _sigmoid(jnp.dot(xc_bf, wx_ref[...], preferred_element_type=F32) + bx_ref[...])
    z = -lam_ref[...]
    softplus = jnp.maximum(z, 0.0) + jnp.log1p(jnp.exp(-jnp.abs(z)))
    log_a = (-LRU_C) * rg * softplus
    a = jnp.exp(log_a)
    a_ref[...] = a
    b_ref[...] = jnp.sqrt(-jnp.tanh(log_a) * (a * a + 1.0)) * (ig * xc_ref[...])

    for r in range(1, STRIDE):
        prev = slice((r - 1) * SLAB, r * SLAB)
        cur = slice(r * SLAB, (r + 1) * SLAB)
        a_cur = a_ref[cur, :]
        b_ref[cur, :] = a_cur * b_ref[prev, :] + b_ref[cur, :]
        a_ref[cur, :] = a_cur * a_ref[prev, :]

    last = slice((STRIDE - 1) * SLAB, STRIDE * SLAB)
    ta = a_ref[last, :]
    tb = b_ref[last, :]
    k = 1
    while k < SLAB:
        tb = ta * _shift_rows(tb, k, 0.0) + tb
        ta = ta * _shift_rows(ta, k, 1.0)
        k *= 2
    carry = _shift_rows(tb, 1, 0.0)

    for r in range(STRIDE):
        cur = slice(r * SLAB, (r + 1) * SLAB)
        h = b_ref[cur, :] + a_ref[cur, :] * carry
        g = xg_ref[cur, :].astype(F32)
        gelu = 0.5 * g * (1.0 + jnp.tanh(0.7978845608028654 * (g + 0.044715 * (g * g * g))))
        o_ref[cur, :] = (h * gelu).astype(o_ref.dtype)


def _lru(proj, batch, conv_w, conv_b, wa_bf16, ba, wx_bf16, bx, lam):
    col = lambda off: pl.BlockSpec((SEQ, HEAD_DIM), lambda b, h: (b, off + h))
    vec = lambda rows: pl.BlockSpec((rows, HEAD_DIM), lambda b, h: (0, h))
    mat = pl.BlockSpec((None, HEAD_DIM, HEAD_DIM), lambda b, h: (h, 0, 0))
    return pl.pallas_call(
        _lru_kernel,
        grid=(batch, HEADS),
        in_specs=[col(3 * HEADS), col(4 * HEADS), vec(CONV_W), vec(1), mat, vec(1), mat, vec(1), vec(1)],
        out_specs=pl.BlockSpec((SEQ, HEAD_DIM), lambda b, h: (b, h)),
        out_shape=jax.ShapeDtypeStruct((batch * SEQ, D_MODEL), BF16),
        scratch_shapes=[pltpu.VMEM((SEQ, HEAD_DIM), F32)] * 3,
        compiler_params=_cparams("parallel", "parallel"),
        name="rg_lru",
    )(proj, proj, conv_w, conv_b.reshape(1, -1), wa_bf16, ba.reshape(1, -1), wx_bf16, bx.reshape(1, -1),
      lam.reshape(1, -1))


MG_TM = 1024
MG_TN = 512


def _merge_kernel(ya_ref, yl_ref, ga_ref, gl_ref, wa_ref, wl_ref, o_ref):
    pa = jnp.dot(ya_ref[...], wa_ref[...], preferred_element_type=F32)
    pl_ = jnp.dot(yl_ref[...], wl_ref[...], preferred_element_type=F32)
    merged = _sigmoid(ga_ref[...].astype(F32)) * pa + _sigmoid(gl_ref[...].astype(F32)) * pl_
    o_ref[...] = merged.astype(o_ref.dtype)


def _merge(y_attn, y_lru, proj, w_pa, w_pl):
    t = y_attn.shape[0]
    nj = D_MODEL // MG_TN
    act = pl.BlockSpec((MG_TM, D_MODEL), lambda i, j: (i, 0))
    gate = lambda off: pl.BlockSpec((MG_TM, MG_TN), lambda i, j: (i, off * nj + j))
    w = pl.BlockSpec((D_MODEL, MG_TN), lambda i, j: (0, j))
    return pl.pallas_call(
        _merge_kernel,
        grid=(t // MG_TM, nj),
        in_specs=[act, act, gate(5), gate(6), w, w],
        out_specs=pl.BlockSpec((MG_TM, MG_TN), lambda i, j: (i, j)),
        out_shape=jax.ShapeDtypeStruct((t, D_MODEL), BF16),
        compiler_params=_cparams("parallel", "arbitrary"),
        name="gated_merge",
    )(y_attn, y_lru, proj, proj, w_pa, w_pl)


OUT_TM = 512


def _out_proj_kernel(x_ref, m_ref, w_ref, o_ref):
    y = jnp.dot(m_ref[...], w_ref[...], preferred_element_type=F32)
    for s in range(OUT_TM // SLAB):
        o_ref[s * SLAB:(s + 1) * SLAB, :] = x_ref[:, s * D_MODEL:(s + 1) * D_MODEL] + y[s * SLAB:(s + 1) * SLAB, :]


def _out_proj(x, merged, w_out):
    b = x.shape[0]
    per_b = SEQ // OUT_TM
    xv = x.reshape(b, SLAB, STRIDE * D_MODEL)
    return pl.pallas_call(
        _out_proj_kernel,
        grid=(b, per_b),
        in_specs=[
            pl.BlockSpec((None, SLAB, (OUT_TM // SLAB) * D_MODEL), lambda bi, i: (bi, 0, i)),
            pl.BlockSpec((OUT_TM, D_MODEL), lambda bi, i: (bi * per_b + i, 0)),
            pl.BlockSpec((D_MODEL, D_MODEL), lambda bi, i: (0, 0)),
        ],
        out_specs=pl.BlockSpec((OUT_TM, D_MODEL), lambda bi, i: (bi * per_b + i, 0)),
        out_shape=jax.ShapeDtypeStruct((b * SEQ, D_MODEL), F32),
        compiler_params=_cparams("parallel", "parallel"),
        name="out_proj_residual",
    )(xv, merged, w_out)


MLP_TM = 512
MLP_TF = 512


def _mlp_kernel(h_ref, g_ref, wu_ref, wd_ref, gf_ref, o_ref, hn_ref, acc_ref):
    f = pl.program_id(2)

    @pl.when(f == 0)
    def _():
        h = h_ref[...]
        ms = jnp.mean(h * h, axis=-1, keepdims=True)
        hn_ref[...] = (h * lax.rsqrt(ms + EPS) * g_ref[...]).astype(BF16)

    up = jnp.dot(hn_ref[...], wu_ref[...], preferred_element_type=F32)
    hid = jnp.square(jnp.maximum(up, 0.0)).astype(BF16)
    part = jnp.dot(hid, wd_ref[...], preferred_element_type=F32)

    @pl.when(f == 0)
    def _():
        acc_ref[...] = part

    @pl.when(f > 0)
    def _():
        acc_ref[...] += part

    @pl.when(f == pl.num_programs(2) - 1)
    def _():
        h2 = h_ref[...] + acc_ref[...]
        ms = jnp.mean(h2 * h2, axis=-1, keepdims=True)
        y = h2 * lax.rsqrt(ms + EPS) * gf_ref[...]
        for s in range(MLP_TM // SLAB):
            o_ref[:, s * D_MODEL:(s + 1) * D_MODEL] = y[s * SLAB:(s + 1) * SLAB, :]


def _mlp(h1, batch, g_mlp, w_up, w_down, g_final):
    per_b = SEQ // MLP_TM
    out = pl.pallas_call(
        _mlp_kernel,
        grid=(batch, per_b, D_FF // MLP_TF),
        in_specs=[
            pl.BlockSpec((MLP_TM, D_MODEL), lambda bi, i, f: (bi * per_b + i, 0)),
            pl.BlockSpec((1, D_MODEL), lambda bi, i, f: (0, 0)),
            pl.BlockSpec((D_MODEL, MLP_TF), lambda bi, i, f: (0, f)),
            pl.BlockSpec((MLP_TF, D_MODEL), lambda bi, i, f: (f, 0)),
            pl.BlockSpec((1, D_MODEL), lambda bi, i, f: (0, 0)),
        ],
        out_specs=pl.BlockSpec((None, SLAB, (MLP_TM // SLAB) * D_MODEL), lambda bi, i, f: (bi, 0, i)),
        out_shape=jax.ShapeDtypeStruct((batch, SLAB, STRIDE * D_MODEL), F32),
        scratch_shapes=[pltpu.VMEM((MLP_TM, D_MODEL), BF16), pltpu.VMEM((MLP_TM, D_MODEL), F32)],
        compiler_params=_cparams("parallel", "parallel", "arbitrary"),
        name="mlp_final_norm",
    )(h1, g_mlp.reshape(1, -1), w_up, w_down, g_final.reshape(1, -1))
    return out.reshape(batch, SEQ, D_MODEL)


def kernel(x, norm_mix_g, w_in, conv_w, conv_b, lru_wa, lru_ba, lru_wx, lru_bx, lru_lambda,
           w_proj_attn, w_proj_lru, w_out, norm_mlp_g, w_up, w_down, norm_final_g):
    assert w_in.shape[0] == 1, "single-layer block"
    batch = x.shape[0]
    bf = lambda w: w.astype(BF16)
    proj = _in_proj(x, norm_mix_g[0], bf(w_in[0]))
    y_attn = _attention(proj, batch)
    y_lru = _lru(proj, batch, conv_w[0], conv_b[0], bf(lru_wa[0]), lru_ba[0], bf(lru_wx[0]), lru_bx[0],
                 lru_lambda[0])
    merged = _merge(y_attn, y_lru, proj, bf(w_proj_attn[0]), bf(w_proj_lru[0]))
    h1 = _out_proj(x, merged, bf(w_out[0]))
    return _mlp(h1, batch, norm_mlp_g[0], bf(w_up[0]), bf(w_down[0]), norm_final_g)
```

```python
import jax
import jax.numpy as jnp
import numpy as np
from jax import lax
from jax.experimental import pallas as pl
from jax.experimental.pallas import tpu as pltpu

D_MODEL = 2048
SEQ = 2048
HEADS = 16
HEAD_DIM = 128
D_FF = 4 * D_MODEL
IN_COLS = 7 * D_MODEL
CONV_W = 4
LRU_C = 8.0
EPS = 1e-6

STRIDE = 16
SLAB = SEQ // STRIDE
HALF = STRIDE // 2
MASKED = -1e30

V7X_VMEM_BYTES = 64 * 1024 * 1024
VMEM_LIMIT = V7X_VMEM_BYTES - 8 * 1024 * 1024

BF16 = jnp.bfloat16
F32 = jnp.float32


def _cparams(*sem):
    return pltpu.CompilerParams(dimension_semantics=sem, vmem_limit_bytes=VMEM_LIMIT)


IN_TM = HALF * SLAB
IN_TN = 1024


def _in_proj_kernel(x_ref, g_ref, w_ref, o_ref, xn_ref, xp_ref):
    @pl.when(pl.program_id(2) == 0)
    def _():
        for s in range(HALF):
            xp_ref[s * SLAB:(s + 1) * SLAB, :] = x_ref[:, s, :]
        xs = xp_ref[...]
        ms = jnp.mean(xs * xs, axis=-1, keepdims=True)
        xn_ref[...] = (xs * lax.rsqrt(ms + EPS) * g_ref[...]).astype(BF16)

    o_ref[...] = jnp.dot(xn_ref[...], w_ref[...], preferred_element_type=F32).astype(o_ref.dtype)


def _in_proj(x4, g, w_bf16):
    b = x4.shape[0]
    return pl.pallas_call(
        _in_proj_kernel,
        grid=(b, 2, IN_COLS // IN_TN),
        in_specs=[
            pl.BlockSpec((None, SLAB, HALF, D_MODEL), lambda bi, i, j: (bi, 0, i, 0)),
            pl.BlockSpec((1, D_MODEL), lambda bi, i, j: (0, 0)),
            pl.BlockSpec((D_MODEL, IN_TN), lambda bi, i, j: (0, j)),
        ],
        out_specs=pl.BlockSpec((IN_TM, IN_TN), lambda bi, i, j: (bi * 2 + i, j)),
        out_shape=jax.ShapeDtypeStruct((b * SEQ, IN_COLS), BF16),
        scratch_shapes=[pltpu.VMEM((IN_TM, D_MODEL), BF16), pltpu.VMEM((IN_TM, D_MODEL), F32)],
        compiler_params=_cparams("parallel", "parallel", "arbitrary"),
        name="in_proj",
    )(x4, g.reshape(1, D_MODEL), w_bf16)


ATTN_SCALE = HEAD_DIM ** -0.5
LOG2E = 1.4426950408889634


def _bias_tables():
    slopes = 2.0 ** (-8.0 * np.arange(1, HEADS + 1, dtype=np.float64) / HEADS)

    def table(tq, tk, span):
        dist = (tq[:, None] - tk[None, :]).astype(np.float64)
        ok = (dist >= 0) & (dist <= span)
        bias = -slopes[:, None, None] * dist[None] / ATTN_SCALE
        return np.where(ok[None], bias, MASKED).astype(np.float32)

    u = np.arange(SLAB)
    b16 = table(16 * u, 16 * u, 16 * 128)
    m4, ul = np.meshgrid(np.arange(4), np.arange(32), indexing="ij")
    m4k, ulk = np.meshgrid(np.arange(4), np.arange(64), indexing="ij")
    tk4 = (16 * ulk + 4 * m4k).reshape(-1)
    b4_first = table((16 * ul + 4 * m4).reshape(-1), tk4, 4 * 128)
    b4 = table((16 * (32 + ul) + 4 * m4).reshape(-1), tk4, 4 * 128)
    r1, ul1 = np.meshgrid(np.arange(16), np.arange(16), indexing="ij")
    r1k, ul1k = np.meshgrid(np.arange(16), np.arange(32), indexing="ij")
    tk1 = (16 * ul1k + r1k).reshape(-1)
    b1_first = table((16 * ul1 + r1).reshape(-1), tk1, 128)
    b1 = table((16 * (16 + ul1) + r1).reshape(-1), tk1, 128)
    return b16, b4_first, b4, b1_first, b1


def _rows(ref, starts, size):
    return jnp.concatenate([ref[s:s + size, :] for s in starts], axis=0)


def _scatter(ref, starts, size, val):
    for i, st in enumerate(starts):
        ref[st:st + size, :] = val[i * size:(i + 1) * size, :]


def _d4_blocks():
    for r4 in range(4):
        slabs = [(r4 + 4 * m) * SLAB for m in range(4)]
        for n in range(4):
            yield [b + 32 * n for b in slabs], [b + 32 * max(n - 1, 0) for b in slabs], n == 0


def _d1_blocks():
    for n in range(SLAB // 16):
        yield ([r * SLAB + 16 * n for r in range(STRIDE)],
               [r * SLAB + 16 * max(n - 1, 0) for r in range(STRIDE)], n == 0)


def _attn_kernel(q_ref, k_ref, v_ref, b16_ref, b4f_ref, b4_ref, b1f_ref, b1_ref, o_ref,
                 s16, s4, s1, p16, p4, p1, a16, a4, linv):
    def qk(q, k):
        return lax.dot_general(q, k, (((1,), (1,)), ((), ())), preferred_element_type=F32)

    for r in range(STRIDE):
        rows = slice(r * SLAB, (r + 1) * SLAB)
        s16[rows, :] = qk(q_ref[rows, :], k_ref[rows, :]) + b16_ref[...]
    for q_st, k_st, first in _d4_blocks():
        s = qk(_rows(q_ref, q_st, 32), _rows(k_ref, k_st, 64)) + (b4f_ref if first else b4_ref)[...]
        _scatter(s4, q_st, 32, s)
    for q_st, k_st, first in _d1_blocks():
        s = qk(_rows(q_ref, q_st, 16), _rows(k_ref, k_st, 32)) + (b1f_ref if first else b1_ref)[...]
        _scatter(s1, q_st, 16, s)

    c = ATTN_SCALE * LOG2E
    for r in range(STRIDE):
        rows = slice(r * SLAB, (r + 1) * SLAB)
        x16, x4, x1 = s16[rows, :], s4[rows, :], s1[rows, :]
        m = jnp.maximum(jnp.maximum(jnp.max(x16, axis=-1, keepdims=True), jnp.max(x4, axis=-1, keepdims=True)),
                        jnp.max(x1, axis=-1, keepdims=True))
        e16 = jnp.exp2((x16 - m) * c)
        e4 = jnp.exp2((x4 - m) * c)
        e1 = jnp.exp2((x1 - m) * c)
        l = (jnp.sum(e16, axis=-1, keepdims=True) + jnp.sum(e4, axis=-1, keepdims=True)
             + jnp.sum(e1, axis=-1, keepdims=True))
        p16[rows, :] = e16.astype(BF16)
        p4[rows, :] = e4.astype(BF16)
        p1[rows, :] = e1.astype(BF16)
        linv[rows, :] = jnp.broadcast_to(1.0 / l, (SLAB, HEAD_DIM))

    for r in range(STRIDE):
        rows = slice(r * SLAB, (r + 1) * SLAB)
        a16[rows, :] = jnp.dot(p16[rows, :], v_ref[rows, :], preferred_element_type=F32)
    for q_st, k_st, _ in _d4_blocks():
        _scatter(a4, q_st, 32, jnp.dot(_rows(p4, q_st, 32), _rows(v_ref, k_st, 64), preferred_element_type=F32))
    for q_st, k_st, _ in _d1_blocks():
        acc = jnp.dot(_rows(p1, q_st, 16), _rows(v_ref, k_st, 32), preferred_element_type=F32)
        acc = acc + _rows(a16, q_st, 16) + _rows(a4, q_st, 16)
        _scatter(o_ref, q_st, 16, (acc * _rows(linv, q_st, 16)).astype(o_ref.dtype))


def _attention(proj, batch):
    tables = [jnp.asarray(t) for t in _bias_tables()]
    per_head = lambda t: pl.BlockSpec((None,) + t.shape[1:], lambda h, b: (h, 0, 0))
    head_block = lambda off: pl.BlockSpec((SEQ, HEAD_DIM), lambda h, b: (b, off + h))
    return pl.pallas_call(
        _attn_kernel,
        grid=(HEADS, batch),
        in_specs=[head_block(0), head_block(HEADS), head_block(2 * HEADS)] + [per_head(t) for t in tables],
        out_specs=pl.BlockSpec((SEQ, HEAD_DIM), lambda h, b: (b, h)),
        out_shape=jax.ShapeDtypeStruct((batch * SEQ, D_MODEL), BF16),
        scratch_shapes=[pltpu.VMEM((SEQ, 128), F32), pltpu.VMEM((SEQ, 256), F32), pltpu.VMEM((SEQ, 512), F32),
                        pltpu.VMEM((SEQ, 128), BF16), pltpu.VMEM((SEQ, 256), BF16), pltpu.VMEM((SEQ, 512), BF16),
                        pltpu.VMEM((SEQ, HEAD_DIM), F32), pltpu.VMEM((SEQ, HEAD_DIM), F32),
                        pltpu.VMEM((SEQ, HEAD_DIM), F32)],
        compiler_params=_cparams("parallel", "parallel"),
        name="dilated_attention",
    )(proj, proj, proj, *tables)


def _shift_rows(x, k, fill):
    n = x.shape[0]
    if k % 8 == 0:
        return jnp.concatenate([jnp.full((k, x.shape[1]), fill, x.dtype), x[:n - k, :]], axis=0)
    rolled = pltpu.roll(x, k, axis=0)
    row = lax.broadcasted_iota(jnp.int32, x.shape, 0)
    return jnp.where(row < k, fill, rolled)


def _sigmoid(x):
    return 1.0 / (1.0 + jnp.exp(-x))


def _lru_kernel(xr_ref, xg_ref, cw_ref, cb_ref, wa_ref, ba_ref, wx_ref, bx_ref, lam_ref,
                o_ref, xc_ref, a_ref, b_ref):
    cw = cw_ref[...]
    cb = cb_ref[...]

    def tap(r):
        x = xr_ref[(r % STRIDE) * SLAB:(r % STRIDE + 1) * SLAB, :].astype(F32)
        return x if r >= 0 else _shift_rows(x, 1, 0.0)

    for r in range(STRIDE):
        xc = cb + cw[3:4, :] * tap(r)
        for j in range(1, CONV_W):
            xc = xc + cw[3 - j:4 - j, :] * tap(r - j)
        xc_ref[r * SLAB:(r + 1) * SLAB, :] = xc

    xc_bf = xc_ref[...].astype(BF16)
    rg = _sigmoid(jnp.dot(xc_bf, wa_ref[...], preferred_element_type=F32) + ba_ref[...])
    ig = _sigmoid(jnp.dot(xc_bf, wx_ref[...], preferred_element_type=F32) + bx_ref[...])
    z = -lam_ref[...]
    softplus = jnp.maximum(z, 0.0) + jnp.log1p(jnp.exp(-jnp.abs(z)))
    log_a = (-LRU_C) * rg * softplus
    a = jnp.exp(log_a)
    a_ref[...] = a
    b_ref[...] = jnp.sqrt(-jnp.tanh(log_a) * (a * a + 1.0)) * (ig * xc_ref[...])

    for r in range(1, STRIDE):
        prev = slice((r - 1) * SLAB, r * SLAB)
        cur = slice(r * SLAB, (r + 1) * SLAB)
        a_cur = a_ref[cur, :]
        b_ref[cur, :] = a_cur * b_ref[prev, :] + b_ref[cur, :]
        a_ref[cur, :] = a_cur * a_ref[prev, :]

    last = slice((STRIDE - 1) * SLAB, STRIDE * SLAB)
    ta = a_ref[last, :]
    tb = b_ref[last, :]
    k = 1
    while k < SLAB:
        tb = ta * _shift_rows(tb, k, 0.0) + tb
        ta = ta * _shift_rows(ta, k, 1.0)
        k *= 2
    carry = _shift_rows(tb, 1, 0.0)

    for r in range(STRIDE):
        cur = slice(r * SLAB, (r + 1) * SLAB)
        h = b_ref[cur, :] + a_ref[cur, :] * carry
        g = xg_ref[cur, :].astype(F32)
        gelu = 0.5 * g * (1.0 + jnp.tanh(0.7978845608028654 * (g + 0.044715 * (g * g * g))))
        o_ref[cur, :] = (h * gelu).astype(o_ref.dtype)


def _lru(proj, batch, conv_w, conv_b, wa_bf16, ba, wx_bf16, bx, lam):
    col = lambda off: pl.BlockSpec((SEQ, HEAD_DIM), lambda b, h: (b, off + h))
    vec = lambda rows: pl.BlockSpec((rows, HEAD_DIM), lambda b, h: (0, h))
    mat = pl.BlockSpec((None, HEAD_DIM, HEAD_DIM), lambda b, h: (h, 0, 0))
    return pl.pallas_call(
        _lru_kernel,
        grid=(batch, HEADS),
        in_specs=[col(3 * HEADS), col(4 * HEADS), vec(CONV_W), vec(1), mat, vec(1), mat, vec(1), vec(1)],
        out_specs=pl.BlockSpec((SEQ, HEAD_DIM), lambda b, h: (b, h)),
        out_shape=jax.ShapeDtypeStruct((batch * SEQ, D_MODEL), BF16),
        scratch_shapes=[pltpu.VMEM((SEQ, HEAD_DIM), F32)] * 3,
        compiler_params=_cparams("parallel", "parallel"),
        name="rg_lru",
    )(proj, proj, conv_w, conv_b.reshape(1, -1), wa_bf16, ba.reshape(1, -1), wx_bf16, bx.reshape(1, -1),
      lam.reshape(1, -1))


MG_TM = 1024
MG_TN = 512


def _merge_kernel(ya_ref, yl_ref, ga_ref, gl_ref, wa_ref, wl_ref, o_ref):
    pa = jnp.dot(ya_ref[...], wa_ref[...], preferred_element_type=F32)
    pl_ = jnp.dot(yl_ref[...], wl_ref[...], preferred_element_type=F32)
    merged = _sigmoid(ga_ref[...].astype(F32)) * pa + _sigmoid(gl_ref[...].astype(F32)) * pl_
    o_ref[...] = merged.astype(o_ref.dtype)


def _merge(y_attn, y_lru, proj, w_pa, w_pl):
    t = y_attn.shape[0]
    nj = D_MODEL // MG_TN
    act = pl.BlockSpec((MG_TM, D_MODEL), lambda i, j: (i, 0))
    gate = lambda off: pl.BlockSpec((MG_TM, MG_TN), lambda i, j: (i, off * nj + j))
    w = pl.BlockSpec((D_MODEL, MG_TN), lambda i, j: (0, j))
    return pl.pallas_call(
        _merge_kernel,
        grid=(t // MG_TM, nj),
        in_specs=[act, act, gate(5), gate(6), w, w],
        out_specs=pl.BlockSpec((MG_TM, MG_TN), lambda i, j: (i, j)),
        out_shape=jax.ShapeDtypeStruct((t, D_MODEL), BF16),
        compiler_params=_cparams("parallel", "arbitrary"),
        name="gated_merge",
    )(y_attn, y_lru, proj, proj, w_pa, w_pl)


OUT_TN = 1024


def _out_proj_kernel(x_ref, m_ref, w_ref, o_ref):
    y = jnp.dot(m_ref[...], w_ref[...], preferred_element_type=F32)
    for s in range(HALF):
        o_ref[:, s, :] = x_ref[:, s, :] + y[s * SLAB:(s + 1) * SLAB, :]


def _out_proj(x4, merged, w_out):
    b = x4.shape[0]
    tok = pl.BlockSpec((None, SLAB, HALF, OUT_TN), lambda bi, i, j: (bi, 0, i, j))
    return pl.pallas_call(
        _out_proj_kernel,
        grid=(b, 2, D_MODEL // OUT_TN),
        in_specs=[
            tok,
            pl.BlockSpec((HALF * SLAB, D_MODEL), lambda bi, i, j: (bi * 2 + i, 0)),
            pl.BlockSpec((D_MODEL, OUT_TN), lambda bi, i, j: (0, j)),
        ],
        out_specs=tok,
        out_shape=jax.ShapeDtypeStruct(x4.shape, F32),
        compiler_params=_cparams("parallel", "parallel", "arbitrary"),
        name="out_proj_residual",
    )(x4, merged, w_out)


MLP_TM = 1024
MLP_TF = 512


def _mlp_kernel(h_ref, g_ref, wu_ref, wd_ref, gf_ref, o_ref, hn_ref):
    f = pl.program_id(1)

    @pl.when(f == 0)
    def _():
        h = h_ref[...]
        ms = jnp.mean(h * h, axis=-1, keepdims=True)
        hn_ref[...] = (h * lax.rsqrt(ms + EPS) * g_ref[...]).astype(BF16)
        o_ref[...] = jnp.zeros_like(o_ref)

    up = jnp.dot(hn_ref[...], wu_ref[...], preferred_element_type=F32)
    hid = jnp.square(jnp.maximum(up, 0.0)).astype(BF16)
    o_ref[...] += jnp.dot(hid, wd_ref[...], preferred_element_type=F32)

    @pl.when(f == pl.num_programs(1) - 1)
    def _():
        h2 = h_ref[...] + o_ref[...]
        ms = jnp.mean(h2 * h2, axis=-1, keepdims=True)
        o_ref[...] = h2 * lax.rsqrt(ms + EPS) * gf_ref[...]


def _mlp(h1, g_mlp, w_up, w_down, g_final):
    t = h1.shape[0]
    return pl.pallas_call(
        _mlp_kernel,
        grid=(t // MLP_TM, D_FF // MLP_TF),
        in_specs=[
            pl.BlockSpec((MLP_TM, D_MODEL), lambda i, f: (i, 0)),
            pl.BlockSpec((1, D_MODEL), lambda i, f: (0, 0)),
            pl.BlockSpec((D_MODEL, MLP_TF), lambda i, f: (0, f)),
            pl.BlockSpec((MLP_TF, D_MODEL), lambda i, f: (f, 0)),
            pl.BlockSpec((1, D_MODEL), lambda i, f: (0, 0)),
        ],
        out_specs=pl.BlockSpec((MLP_TM, D_MODEL), lambda i, f: (i, 0)),
        out_shape=jax.ShapeDtypeStruct((t, D_MODEL), F32),
        scratch_shapes=[pltpu.VMEM((MLP_TM, D_MODEL), BF16)],
        compiler_params=_cparams("parallel", "arbitrary"),
        name="mlp_final_norm",
    )(h1, g_mlp.reshape(1, -1), w_up, w_down, g_final.reshape(1, -1))


def kernel(x, norm_mix_g, w_in, conv_w, conv_b, lru_wa, lru_ba, lru_wx, lru_bx, lru_lambda,
           w_proj_attn, w_proj_lru, w_out, norm_mlp_g, w_up, w_down, norm_final_g):
    assert w_in.shape[0] == 1, "single-layer block"
    batch = x.shape[0]
    bf = lambda w: w.astype(BF16)
    x4 = x.reshape(batch, SLAB, STRIDE, D_MODEL)
    proj = _in_proj(x4, norm_mix_g[0], bf(w_in[0]))
    y_attn = _attention(proj, batch)
    y_lru = _lru(proj, batch, conv_w[0], conv_b[0], bf(lru_wa[0]), lru_ba[0], bf(lru_wx[0]), lru_bx[0],
                 lru_lambda[0])
    merged = _merge(y_attn, y_lru, proj, bf(w_proj_attn[0]), bf(w_proj_lru[0]))
    h1 = _out_proj(x4, merged, bf(w_out[0]))
    out = _mlp(h1.reshape(batch * SEQ, D_MODEL), norm_mlp_g[0], bf(w_up[0]), bf(w_down[0]), norm_final_g)
    return out.reshape(batch, SEQ, D_MODEL)
```

```python
import jax
import jax.numpy as jnp
import numpy as np
from jax import lax
from jax.experimental import pallas as pl
from jax.experimental.pallas import tpu as pltpu

D_MODEL = 2048
SEQ = 2048
HEADS = 16
HEAD_DIM = 128
D_FF = 4 * D_MODEL
IN_COLS = 7 * D_MODEL
CONV_W = 4
LRU_C = 8.0
EPS = 1e-6

STRIDE = 16
SLAB = SEQ // STRIDE
HALF = STRIDE // 2
MASKED = -1e30
LOG2E = 1.4426950408889634
Q_SCALE = HEAD_DIM ** -0.5 * LOG2E

V7X_VMEM_BYTES = 64 * 1024 * 1024
VMEM_LIMIT = V7X_VMEM_BYTES - 8 * 1024 * 1024

BF16 = jnp.bfloat16
F32 = jnp.float32


def _cparams(*sem):
    return pltpu.CompilerParams(dimension_semantics=sem, vmem_limit_bytes=VMEM_LIMIT)


def _sigmoid(x):
    return 1.0 / (1.0 + jnp.exp2(x * (-LOG2E)))


IN_TM = HALF * SLAB
IN_TN = 1024


def _in_proj_kernel(x_ref, g_ref, w_ref, o_ref, xn_ref, xp_ref):
    @pl.when(pl.program_id(2) == 0)
    def _():
        for s in range(HALF):
            xp_ref[s * SLAB:(s + 1) * SLAB, :] = x_ref[:, s, :]
        xs = xp_ref[...]
        ms = jnp.mean(xs * xs, axis=-1, keepdims=True)
        xn_ref[...] = (xs * lax.rsqrt(ms + EPS) * g_ref[...]).astype(BF16)

    q_cols = pl.program_id(2) < D_MODEL // IN_TN
    out_scale = jnp.where(q_cols, Q_SCALE, 1.0).astype(F32)
    y = jnp.dot(xn_ref[...], w_ref[...].astype(BF16), preferred_element_type=F32)
    o_ref[...] = (y * out_scale).astype(o_ref.dtype)


def _in_proj(x4, g, w):
    b = x4.shape[0]
    return pl.pallas_call(
        _in_proj_kernel,
        grid=(b, 2, IN_COLS // IN_TN),
        in_specs=[
            pl.BlockSpec((None, SLAB, HALF, D_MODEL), lambda bi, i, j: (bi, 0, i, 0)),
            pl.BlockSpec((1, D_MODEL), lambda bi, i, j: (0, 0)),
            pl.BlockSpec((D_MODEL, IN_TN), lambda bi, i, j: (0, j)),
        ],
        out_specs=pl.BlockSpec((IN_TM, IN_TN), lambda bi, i, j: (bi * 2 + i, j)),
        out_shape=jax.ShapeDtypeStruct((b * SEQ, IN_COLS), BF16),
        scratch_shapes=[pltpu.VMEM((IN_TM, D_MODEL), BF16), pltpu.VMEM((IN_TM, D_MODEL), F32)],
        compiler_params=_cparams("parallel", "parallel", "arbitrary"),
        name="in_proj",
    )(x4, g.reshape(1, D_MODEL), w)


def _bias_tables():
    slopes = 2.0 ** (-8.0 * np.arange(1, HEADS + 1, dtype=np.float64) / HEADS)

    def table(tq, tk, span):
        dist = (tq[:, None] - tk[None, :]).astype(np.float64)
        ok = (dist >= 0) & (dist <= span)
        bias = -slopes[:, None, None] * dist[None] * LOG2E
        return np.where(ok[None], bias, MASKED).astype(np.float32)

    u = np.arange(SLAB)
    b16 = table(16 * u, 16 * u, 16 * 128)
    m4, ul = np.meshgrid(np.arange(4), np.arange(32), indexing="ij")
    m4k, ulk = np.meshgrid(np.arange(4), np.arange(64), indexing="ij")
    tk4 = (16 * ulk + 4 * m4k).reshape(-1)
    b4_first = table((16 * ul + 4 * m4).reshape(-1), tk4, 4 * 128)
    b4 = table((16 * (32 + ul) + 4 * m4).reshape(-1), tk4, 4 * 128)
    r1, ul1 = np.meshgrid(np.arange(16), np.arange(16), indexing="ij")
    r1k, ul1k = np.meshgrid(np.arange(16), np.arange(32), indexing="ij")
    tk1 = (16 * ul1k + r1k).reshape(-1)
    b1_first = table((16 * ul1 + r1).reshape(-1), tk1, 128)
    b1 = table((16 * (16 + ul1) + r1).reshape(-1), tk1, 128)
    return b16, b4_first, b4, b1_first, b1


def _rows(ref, starts, size):
    return jnp.concatenate([ref[s:s + size, :] for s in starts], axis=0)


def _scatter(ref, starts, size, val):
    for i, st in enumerate(starts):
        ref[st:st + size, :] = val[i * size:(i + 1) * size, :]


def _d4_blocks():
    for r4 in range(4):
        slabs = [(r4 + 4 * m) * SLAB for m in range(4)]
        for n in range(4):
            yield [b + 32 * n for b in slabs], [b + 32 * max(n - 1, 0) for b in slabs], n == 0


def _d1_blocks():
    for n in range(SLAB // 16):
        yield ([r * SLAB + 16 * n for r in range(STRIDE)],
               [r * SLAB + 16 * max(n - 1, 0) for r in range(STRIDE)], n == 0)


def _attn_body(q_ref, k_ref, v_ref, b16_ref, b4f_ref, b4_ref, b1f_ref, b1_ref, o_ref,
               s16, s4, s1, p16, p4, p1, a16, a4, linv):
    def qk(q, k):
        return lax.dot_general(q, k, (((1,), (1,)), ((), ())), preferred_element_type=F32)

    for r in range(STRIDE):
        rows = slice(r * SLAB, (r + 1) * SLAB)
        s16[rows, :] = qk(q_ref[rows, :], k_ref[rows, :]) + b16_ref[...]
    for q_st, k_st, first in _d4_blocks():
        s = qk(_rows(q_ref, q_st, 32), _rows(k_ref, k_st, 64)) + (b4f_ref if first else b4_ref)[...]
        _scatter(s4, q_st, 32, s)
    for q_st, k_st, first in _d1_blocks():
        s = qk(_rows(q_ref, q_st, 16), _rows(k_ref, k_st, 32)) + (b1f_ref if first else b1_ref)[...]
        _scatter(s1, q_st, 16, s)

    for r in range(STRIDE):
        rows = slice(r * SLAB, (r + 1) * SLAB)
        x16, x4, x1 = s16[rows, :], s4[rows, :], s1[rows, :]
        m = jnp.maximum(jnp.maximum(jnp.max(x16, axis=-1, keepdims=True), jnp.max(x4, axis=-1, keepdims=True)),
                        jnp.max(x1, axis=-1, keepdims=True))
        e16 = jnp.exp2(x16 - m)
        e4 = jnp.exp2(x4 - m)
        e1 = jnp.exp2(x1 - m)
        l = (jnp.sum(e16, axis=-1, keepdims=True) + jnp.sum(e4, axis=-1, keepdims=True)
             + jnp.sum(e1, axis=-1, keepdims=True))
        p16[rows, :] = e16.astype(BF16)
        p4[rows, :] = e4.astype(BF16)
        p1[rows, :] = e1.astype(BF16)
        linv[rows, :] = jnp.broadcast_to(1.0 / l, (SLAB, HEAD_DIM))

    for r in range(STRIDE):
        rows = slice(r * SLAB, (r + 1) * SLAB)
        a16[rows, :] = jnp.dot(p16[rows, :], v_ref[rows, :], preferred_element_type=F32)
    for q_st, k_st, _ in _d4_blocks():
        _scatter(a4, q_st, 32, jnp.dot(_rows(p4, q_st, 32), _rows(v_ref, k_st, 64), preferred_element_type=F32))
    for q_st, k_st, _ in _d1_blocks():
        acc = jnp.dot(_rows(p1, q_st, 16), _rows(v_ref, k_st, 32), preferred_element_type=F32)
        acc = acc + _rows(a16, q_st, 16) + _rows(a4, q_st, 16)
        _scatter(o_ref, q_st, 16, (acc * _rows(linv, q_st, 16)).astype(o_ref.dtype))


def _shift_rows(x, k, fill):
    n = x.shape[0]
    if k % 8 == 0:
        return jnp.concatenate([jnp.full((k, x.shape[1]), fill, x.dtype), x[:n - k, :]], axis=0)
    rolled = pltpu.roll(x, k, axis=0)
    row = lax.broadcasted_iota(jnp.int32, x.shape, 0)
    return jnp.where(row < k, fill, rolled)


def _lru_body(xr_ref, xg_ref, cw_ref, cb_ref, wa_ref, ba_ref, wx_ref, bx_ref, lam_ref,
              o_ref, x32_ref, xc_ref, a_ref, b_ref):
    cw = cw_ref[...]
    cb = cb_ref[...]
    x32_ref[...] = xr_ref[...].astype(F32)

    def tap(r):
        x = x32_ref[(r % STRIDE) * SLAB:(r % STRIDE + 1) * SLAB, :]
        return x if r >= 0 else _shift_rows(x, 1, 0.0)

    for r in range(STRIDE):
        xc = cb + cw[3:4, :] * tap(r)
        for j in range(1, CONV_W):
            xc = xc + cw[3 - j:4 - j, :] * tap(r - j)
        xc_ref[r * SLAB:(r + 1) * SLAB, :] = xc

    xc_bf = xc_ref[...].astype(BF16)
    rg = _sigmoid(jnp.dot(xc_bf, wa_ref[...].astype(BF16), preferred_element_type=F32) + ba_ref[...])
    ig = _sigmoid(jnp.dot(xc_bf, wx_ref[...].astype(BF16), preferred_element_type=F32) + bx_ref[...])
    z = -lam_ref[...]
    softplus = jnp.maximum(z, 0.0) + jnp.log1p(jnp.exp(-jnp.abs(z)))
    log_a = rg * ((-LRU_C) * softplus)
    a = jnp.exp(log_a)
    a_ref[...] = a
    m2 = -jnp.tanh(log_a) * (a * a + 1.0)
    mult = jnp.where(m2 == 0.0, 0.0, m2 * lax.rsqrt(m2))
    b_ref[...] = mult * (ig * xc_ref[...])

    for r in range(1, STRIDE):
        prev = slice((r - 1) * SLAB, r * SLAB)
        cur = slice(r * SLAB, (r + 1) * SLAB)
        a_cur = a_ref[cur, :]
        b_ref[cur, :] = a_cur * b_ref[prev, :] + b_ref[cur, :]
        a_ref[cur, :] = a_cur * a_ref[prev, :]

    last = slice((STRIDE - 1) * SLAB, STRIDE * SLAB)
    ta = a_ref[last, :]
    tb = b_ref[last, :]
    k = 1
    while k < SLAB:
        tb = ta * _shift_rows(tb, k, 0.0) + tb
        ta = ta * _shift_rows(ta, k, 1.0)
        k *= 2
    carry = _shift_rows(tb, 1, 0.0)

    for r in range(STRIDE):
        cur = slice(r * SLAB, (r + 1) * SLAB)
        h = b_ref[cur, :] + a_ref[cur, :] * carry
        g = xg_ref[cur, :].astype(F32)
        t = jnp.tanh(g * (0.7978845608028654 + (0.7978845608028654 * 0.044715) * (g * g)))
        o_ref[cur, :] = ((h * g) * (0.5 + 0.5 * t)).astype(o_ref.dtype)


N_ATTN_IN = 8
N_LRU_IN = 9


def _mixers_kernel(*refs):
    attn_in = refs[:N_ATTN_IN]
    lru_in = refs[N_ATTN_IN:N_ATTN_IN + N_LRU_IN]
    o_attn, o_lru = refs[N_ATTN_IN + N_LRU_IN:N_ATTN_IN + N_LRU_IN + 2]
    scratch = refs[N_ATTN_IN + N_LRU_IN + 2:]
    _attn_body(*attn_in, o_attn, *scratch[:9])
    _lru_body(*lru_in, o_lru, *scratch[9:])


def _mixers(proj, batch, conv_w, conv_b, wa, ba, wx, bx, lam):
    tables = [jnp.asarray(t) for t in _bias_tables()]
    per_head = lambda t: pl.BlockSpec((None,) + t.shape[1:], lambda h, b: (h, 0, 0))
    col = lambda off: pl.BlockSpec((SEQ, HEAD_DIM), lambda h, b: (b, off + h))
    vec = lambda rows: pl.BlockSpec((rows, HEAD_DIM), lambda h, b: (0, h))
    mat = pl.BlockSpec((None, HEAD_DIM, HEAD_DIM), lambda h, b: (h, 0, 0))
    out = pl.BlockSpec((SEQ, HEAD_DIM), lambda h, b: (b, h))
    f32_rows = lambda w: pltpu.VMEM((SEQ, w), F32)
    bf16_rows = lambda w: pltpu.VMEM((SEQ, w), BF16)
    return pl.pallas_call(
        _mixers_kernel,
        grid=(HEADS, batch),
        in_specs=[col(0), col(HEADS), col(2 * HEADS)] + [per_head(t) for t in tables]
                 + [col(3 * HEADS), col(4 * HEADS), vec(CONV_W), vec(1), mat, vec(1), mat, vec(1), vec(1)],
        out_specs=[out, out],
        out_shape=[jax.ShapeDtypeStruct((batch * SEQ, D_MODEL), BF16)] * 2,
        scratch_shapes=[f32_rows(128), f32_rows(256), f32_rows(512), bf16_rows(128), bf16_rows(256), bf16_rows(512),
                        f32_rows(HEAD_DIM), f32_rows(HEAD_DIM), f32_rows(HEAD_DIM)]
                       + [f32_rows(HEAD_DIM)] * 4,
        compiler_params=_cparams("parallel", "parallel"),
        name="token_mixers",
    )(proj, proj, proj, *tables, proj, proj, conv_w, conv_b.reshape(1, -1), wa, ba.reshape(1, -1), wx,
      bx.reshape(1, -1), lam.reshape(1, -1))


MG_TM = 1024
MG_TN = 512


def _merge_kernel(ya_ref, yl_ref, ga_ref, gl_ref, wa_ref, wl_ref, o_ref):
    pa = jnp.dot(ya_ref[...], wa_ref[...].astype(BF16), preferred_element_type=F32)
    pl_ = jnp.dot(yl_ref[...], wl_ref[...].astype(BF16), preferred_element_type=F32)
    merged = _sigmoid(ga_ref[...].astype(F32)) * pa + _sigmoid(gl_ref[...].astype(F32)) * pl_
    o_ref[...] = merged.astype(o_ref.dtype)


def _merge(y_attn, y_lru, proj, w_pa, w_pl):
    t = y_attn.shape[0]
    nj = D_MODEL // MG_TN
    act = pl.BlockSpec((MG_TM, D_MODEL), lambda i, j: (i, 0))
    gate = lambda off: pl.BlockSpec((MG_TM, MG_TN), lambda i, j: (i, off * nj + j))
    w = pl.BlockSpec((D_MODEL, MG_TN), lambda i, j: (0, j))
    return pl.pallas_call(
        _merge_kernel,
        grid=(t // MG_TM, nj),
        in_specs=[act, act, gate(5), gate(6), w, w],
        out_specs=pl.BlockSpec((MG_TM, MG_TN), lambda i, j: (i, j)),
        out_shape=jax.ShapeDtypeStruct((t, D_MODEL), BF16),
        compiler_params=_cparams("parallel", "arbitrary"),
        name="gated_merge",
    )(y_attn, y_lru, proj, proj, w_pa, w_pl)


OUT_TN = 1024


def _out_proj_kernel(x_ref, m_ref, w_ref, o_ref):
    y = jnp.dot(m_ref[...], w_ref[...].astype(BF16), preferred_element_type=F32)
    for s in range(HALF):
        o_ref[:, s, :] = x_ref[:, s, :] + y[s * SLAB:(s + 1) * SLAB, :]


def _out_proj(x4, merged, w_out):
    b = x4.shape[0]
    tok = pl.BlockSpec((None, SLAB, HALF, OUT_TN), lambda bi, i, j: (bi, 0, i, j))
    return pl.pallas_call(
        _out_proj_kernel,
        grid=(b, 2, D_MODEL // OUT_TN),
        in_specs=[
            tok,
            pl.BlockSpec((HALF * SLAB, D_MODEL), lambda bi, i, j: (bi * 2 + i, 0)),
            pl.BlockSpec((D_MODEL, OUT_TN), lambda bi, i, j: (0, j)),
        ],
        out_specs=tok,
        out_shape=jax.ShapeDtypeStruct(x4.shape, F32),
        compiler_params=_cparams("parallel", "parallel", "arbitrary"),
        name="out_proj_residual",
    )(x4, merged, w_out)


MLP_TM = 1024
MLP_TF = 512


def _mlp_kernel(h_ref, g_ref, wu_ref, wd_ref, gf_ref, o_ref, hn_ref):
    f = pl.program_id(1)

    @pl.when(f == 0)
    def _():
        h = h_ref[...]
        ms = jnp.mean(h * h, axis=-1, keepdims=True)
        hn_ref[...] = (h * lax.rsqrt(ms + EPS) * g_ref[...]).astype(BF16)
        o_ref[...] = jnp.zeros_like(o_ref)

    up = jnp.dot(hn_ref[...], wu_ref[...], preferred_element_type=F32)
    hid = jnp.square(jnp.maximum(up, 0.0)).astype(BF16)
    o_ref[...] += jnp.dot(hid, wd_ref[...], preferred_element_type=F32)

    @pl.when(f == pl.num_programs(1) - 1)
    def _():
        h2 = h_ref[...] + o_ref[...]
        ms = jnp.mean(h2 * h2, axis=-1, keepdims=True)
        o_ref[...] = h2 * lax.rsqrt(ms + EPS) * gf_ref[...]


def _mlp(h1, g_mlp, w_up, w_down, g_final):
    t = h1.shape[0]
    return pl.pallas_call(
        _mlp_kernel,
        grid=(t // MLP_TM, D_FF // MLP_TF),
        in_specs=[
            pl.BlockSpec((MLP_TM, D_MODEL), lambda i, f: (i, 0)),
            pl.BlockSpec((1, D_MODEL), lambda i, f: (0, 0)),
            pl.BlockSpec((D_MODEL, MLP_TF), lambda i, f: (0, f)),
            pl.BlockSpec((MLP_TF, D_MODEL), lambda i, f: (f, 0)),
            pl.BlockSpec((1, D_MODEL), lambda i, f: (0, 0)),
        ],
        out_specs=pl.BlockSpec((MLP_TM, D_MODEL), lambda i, f: (i, 0)),
        out_shape=jax.ShapeDtypeStruct((t, D_MODEL), F32),
        scratch_shapes=[pltpu.VMEM((MLP_TM, D_MODEL), BF16)],
        compiler_params=_cparams("parallel", "arbitrary"),
        name="mlp_final_norm",
    )(h1, g_mlp.reshape(1, -1), w_up, w_down, g_final.reshape(1, -1))


def kernel(x, norm_mix_g, w_in, conv_w, conv_b, lru_wa, lru_ba, lru_wx, lru_bx, lru_lambda,
           w_proj_attn, w_proj_lru, w_out, norm_mlp_g, w_up, w_down, norm_final_g):
    assert w_in.shape[0] == 1, "single-layer block"
    batch = x.shape[0]
    x4 = x.reshape(batch, SLAB, STRIDE, D_MODEL)
    proj = _in_proj(x4, norm_mix_g[0], w_in[0])
    y_attn, y_lru = _mixers(proj, batch, conv_w[0], conv_b[0], lru_wa[0], lru_ba[0], lru_wx[0], lru_bx[0],
                            lru_lambda[0])
    merged = _merge(y_attn, y_lru, proj, w_proj_attn[0], w_proj_lru[0])
    h1 = _out_proj(x4, merged, w_out[0])
    out = _mlp(h1.reshape(batch * SEQ, D_MODEL), norm_mlp_g[0], w_up[0].astype(BF16), w_down[0].astype(BF16),
               norm_final_g)
    return out.reshape(batch, SEQ, D_MODEL)
```

```python
import jax
import jax.numpy as jnp
import numpy as np
from jax import lax
from jax.experimental import pallas as pl
from jax.experimental.pallas import tpu as pltpu

D_MODEL = 2048
SEQ = 2048
HEADS = 16
HEAD_DIM = 128
D_FF = 4 * D_MODEL
IN_COLS = 7 * D_MODEL
CONV_W = 4
LRU_C = 8.0
EPS = 1e-6

STRIDE = 16
SLAB = SEQ // STRIDE
HALF = STRIDE // 2
MASKED = -1e30
LOG2E = 1.4426950408889634
Q_SCALE = HEAD_DIM ** -0.5 * LOG2E

V7X_VMEM_BYTES = 64 * 1024 * 1024
VMEM_LIMIT = V7X_VMEM_BYTES - 8 * 1024 * 1024

BF16 = jnp.bfloat16
F32 = jnp.float32


def _cparams(*sem):
    return pltpu.CompilerParams(dimension_semantics=sem, vmem_limit_bytes=VMEM_LIMIT)


def _sigmoid(x):
    return 1.0 / (1.0 + jnp.exp2(x * (-LOG2E)))


def _gelu_tanh(x):
    c = 0.7978845608028654
    return x * (0.5 + 0.5 * jnp.tanh(x * (c + (c * 0.044715) * (x * x))))


IN_TM = HALF * SLAB
IN_TN = 1024
LRU_GATE_SPLIT = 4


def _in_proj_kernel(x_ref, g_ref, w_ref, o_ref, xn_ref, xp_ref):
    @pl.when(pl.program_id(2) == 0)
    def _():
        for s in range(HALF):
            xp_ref[s * SLAB:(s + 1) * SLAB, :] = x_ref[:, s, :]
        xs = xp_ref[...]
        ms = jnp.mean(xs * xs, axis=-1, keepdims=True)
        xn_ref[...] = (xs * lax.rsqrt(ms + EPS) * g_ref[...]).astype(BF16)

    j = pl.program_id(2)
    tiles = D_MODEL // IN_TN
    is_q = j < tiles
    is_lru_gate = jnp.logical_and(j >= LRU_GATE_SPLIT * tiles, j < (LRU_GATE_SPLIT + 1) * tiles)

    def project(epilogue):
        y = jnp.dot(xn_ref[...], w_ref[...].astype(BF16), preferred_element_type=F32)
        o_ref[...] = epilogue(y).astype(o_ref.dtype)

    pl.when(is_q)(lambda: project(lambda y: y * Q_SCALE))
    pl.when(is_lru_gate)(lambda: project(_gelu_tanh))
    pl.when(jnp.logical_not(jnp.logical_or(is_q, is_lru_gate)))(lambda: project(lambda y: y))


def _in_proj(x4, g, w):
    b = x4.shape[0]
    return pl.pallas_call(
        _in_proj_kernel,
        grid=(b, 2, IN_COLS // IN_TN),
        in_specs=[
            pl.BlockSpec((None, SLAB, HALF, D_MODEL), lambda bi, i, j: (bi, 0, i, 0)),
            pl.BlockSpec((1, D_MODEL), lambda bi, i, j: (0, 0)),
            pl.BlockSpec((D_MODEL, IN_TN), lambda bi, i, j: (0, j)),
        ],
        out_specs=pl.BlockSpec((IN_TM, IN_TN), lambda bi, i, j: (bi * 2 + i, j)),
        out_shape=jax.ShapeDtypeStruct((b * SEQ, IN_COLS), BF16),
        scratch_shapes=[pltpu.VMEM((IN_TM, D_MODEL), BF16), pltpu.VMEM((IN_TM, D_MODEL), F32)],
        compiler_params=_cparams("parallel", "parallel", "arbitrary"),
        name="in_proj",
    )(x4, g.reshape(1, D_MODEL), w)


def _bias_tables():
    slopes = 2.0 ** (-8.0 * np.arange(1, HEADS + 1, dtype=np.float64) / HEADS)

    def table(tq, tk, span):
        dist = (tq[:, None] - tk[None, :]).astype(np.float64)
        ok = (dist >= 0) & (dist <= span)
        bias = -slopes[:, None, None] * dist[None] * LOG2E
        return np.where(ok[None], bias, MASKED).astype(np.float32)

    u = np.arange(SLAB)
    b16 = table(16 * u, 16 * u, 16 * 128)
    m4, ul = np.meshgrid(np.arange(4), np.arange(32), indexing="ij")
    m4k, ulk = np.meshgrid(np.arange(4), np.arange(64), indexing="ij")
    tk4 = (16 * ulk + 4 * m4k).reshape(-1)
    b4_first = table((16 * ul + 4 * m4).reshape(-1), tk4, 4 * 128)
    b4 = table((16 * (32 + ul) + 4 * m4).reshape(-1), tk4, 4 * 128)
    r1, ul1 = np.meshgrid(np.arange(16), np.arange(D1_Q), indexing="ij")
    r1k, ul1k = np.meshgrid(np.arange(16), np.arange(2 * D1_Q), indexing="ij")
    tk1 = (16 * ul1k + r1k).reshape(-1)
    b1_first = table((16 * ul1 + r1).reshape(-1), tk1, 128)
    b1 = table((16 * (D1_Q + ul1) + r1).reshape(-1), tk1, 128)
    return b16, b4_first, b4, b1_first, b1


def _rows(ref, starts, size):
    return jnp.concatenate([ref[s:s + size, :] for s in starts], axis=0)


def _scatter(ref, starts, size, val):
    for i, st in enumerate(starts):
        ref[st:st + size, :] = val[i * size:(i + 1) * size, :]


def _d4_blocks():
    for r4 in range(4):
        slabs = [(r4 + 4 * m) * SLAB for m in range(4)]
        for n in range(4):
            yield [b + 32 * n for b in slabs], [b + 32 * max(n - 1, 0) for b in slabs], n == 0


D1_Q = 8


def _d1_blocks():
    for n in range(SLAB // D1_Q):
        yield ([r * SLAB + D1_Q * n for r in range(STRIDE)],
               [r * SLAB + D1_Q * max(n - 1, 0) for r in range(STRIDE)], n == 0)


def _attn_body(q_ref, k_ref, v_ref, b16_ref, b4f_ref, b4_ref, b1f_ref, b1_ref, o_ref,
               s16, s4, s1, p16, p4, p1, a16, a4, linv):
    def qk(q, k):
        return lax.dot_general(q, k, (((1,), (1,)), ((), ())), preferred_element_type=F32)

    for r in range(STRIDE):
        rows = slice(r * SLAB, (r + 1) * SLAB)
        s16[rows, :] = qk(q_ref[rows, :], k_ref[rows, :]) + b16_ref[...]
    for q_st, k_st, first in _d4_blocks():
        s = qk(_rows(q_ref, q_st, 32), _rows(k_ref, k_st, 64)) + (b4f_ref if first else b4_ref)[...]
        _scatter(s4, q_st, 32, s)
    for q_st, k_st, first in _d1_blocks():
        s = qk(_rows(q_ref, q_st, D1_Q), _rows(k_ref, k_st, 2 * D1_Q)) + (b1f_ref if first else b1_ref)[...]
        _scatter(s1, q_st, D1_Q, s)

    for r in range(STRIDE):
        rows = slice(r * SLAB, (r + 1) * SLAB)
        x16, x4, x1 = s16[rows, :], s4[rows, :], s1[rows, :]
        m = jnp.maximum(jnp.maximum(jnp.max(x16, axis=-1, keepdims=True), jnp.max(x4, axis=-1, keepdims=True)),
                        jnp.max(x1, axis=-1, keepdims=True))
        e16 = jnp.exp2(x16 - m)
        e4 = jnp.exp2(x4 - m)
        e1 = jnp.exp2(x1 - m)
        l = (jnp.sum(e16, axis=-1, keepdims=True) + jnp.sum(e4, axis=-1, keepdims=True)
             + jnp.sum(e1, axis=-1, keepdims=True))
        p16[rows, :] = e16.astype(BF16)
        p4[rows, :] = e4.astype(BF16)
        p1[rows, :] = e1.astype(BF16)
        linv[rows, :] = jnp.broadcast_to(1.0 / l, (SLAB, HEAD_DIM))

    for r in range(STRIDE):
        rows = slice(r * SLAB, (r + 1) * SLAB)
        a16[rows, :] = jnp.dot(p16[rows, :], v_ref[rows, :], preferred_element_type=F32)
    for q_st, k_st, _ in _d4_blocks():
        _scatter(a4, q_st, 32, jnp.dot(_rows(p4, q_st, 32), _rows(v_ref, k_st, 64), preferred_element_type=F32))
    for q_st, k_st, _ in _d1_blocks():
        acc = jnp.dot(_rows(p1, q_st, D1_Q), _rows(v_ref, k_st, 2 * D1_Q), preferred_element_type=F32)
        acc = acc + _rows(a16, q_st, D1_Q) + _rows(a4, q_st, D1_Q)
        _scatter(o_ref, q_st, D1_Q, (acc * _rows(linv, q_st, D1_Q)).astype(o_ref.dtype))


def _shift_rows(x, k, fill):
    n = x.shape[0]
    if k % 8 == 0:
        return jnp.concatenate([jnp.full((k, x.shape[1]), fill, x.dtype), x[:n - k, :]], axis=0)
    rolled = pltpu.roll(x, k, axis=0)
    row = lax.broadcasted_iota(jnp.int32, x.shape, 0)
    return jnp.where(row < k, fill, rolled)


def _lru_body(xr_ref, gg_ref, cw_ref, cb_ref, wa_ref, ba_ref, wx_ref, bx_ref, lam_ref,
              o_ref, x32_ref, xc_ref, a_ref, b_ref):
    cw = cw_ref[...]
    cb = cb_ref[...]
    x32_ref[...] = xr_ref[...].astype(F32)

    def tap(r):
        x = x32_ref[(r % STRIDE) * SLAB:(r % STRIDE + 1) * SLAB, :]
        return x if r >= 0 else _shift_rows(x, 1, 0.0)

    for r in range(STRIDE):
        xc = cb + cw[3:4, :] * tap(r)
        for j in range(1, CONV_W):
            xc = xc + cw[3 - j:4 - j, :] * tap(r - j)
        xc_ref[r * SLAB:(r + 1) * SLAB, :] = xc

    w_gates = jnp.concatenate([wa_ref[...], wx_ref[...]], axis=1).astype(BF16)
    gates = jnp.dot(xc_ref[...].astype(BF16), w_gates, preferred_element_type=F32)
    rg = _sigmoid(gates[:, :HEAD_DIM] + ba_ref[...])
    ig = _sigmoid(gates[:, HEAD_DIM:] + bx_ref[...])
    z = -lam_ref[...]
    softplus = jnp.maximum(z, 0.0) + jnp.log1p(jnp.exp(-jnp.abs(z)))
    log_a = rg * ((-LRU_C) * softplus)
    a = jnp.exp(log_a)
    a_ref[...] = a
    m2 = -jnp.tanh(log_a) * (a * a + 1.0)
    mult = jnp.where(m2 == 0.0, 0.0, m2 * lax.rsqrt(m2))
    b_ref[...] = mult * (ig * xc_ref[...])

    for r in range(1, STRIDE):
        prev = slice((r - 1) * SLAB, r * SLAB)
        cur = slice(r * SLAB, (r + 1) * SLAB)
        a_cur = a_ref[cur, :]
        b_ref[cur, :] = a_cur * b_ref[prev, :] + b_ref[cur, :]
        a_ref[cur, :] = a_cur * a_ref[prev, :]

    last = slice((STRIDE - 1) * SLAB, STRIDE * SLAB)
    ta = a_ref[last, :]
    tb = b_ref[last, :]
    k = 1
    while k < SLAB:
        tb = ta * _shift_rows(tb, k, 0.0) + tb
        ta = ta * _shift_rows(ta, k, 1.0)
        k *= 2
    carry = _shift_rows(tb, 1, 0.0)

    for r in range(STRIDE):
        cur = slice(r * SLAB, (r + 1) * SLAB)
        h = b_ref[cur, :] + a_ref[cur, :] * carry
        o_ref[cur, :] = (h * gg_ref[cur, :].astype(F32)).astype(o_ref.dtype)


N_ATTN_IN = 8
N_LRU_IN = 9


def _mixers_kernel(*refs):
    attn_in = refs[:N_ATTN_IN]
    lru_in = refs[N_ATTN_IN:N_ATTN_IN + N_LRU_IN]
    o_attn, o_lru = refs[N_ATTN_IN + N_LRU_IN:N_ATTN_IN + N_LRU_IN + 2]
    scratch = refs[N_ATTN_IN + N_LRU_IN + 2:]
    _attn_body(*attn_in, o_attn, *scratch[:9])
    _lru_body(*lru_in, o_lru, *scratch[9:])


def _mixers(proj, batch, conv_w, conv_b, wa, ba, wx, bx, lam):
    tables = [jnp.asarray(t) for t in _bias_tables()]
    per_head = lambda t: pl.BlockSpec((None,) + t.shape[1:], lambda h, b: (h, 0, 0))
    col = lambda off: pl.BlockSpec((SEQ, HEAD_DIM), lambda h, b: (b, off + h))
    vec = lambda rows: pl.BlockSpec((rows, HEAD_DIM), lambda h, b: (0, h))
    mat = pl.BlockSpec((None, HEAD_DIM, HEAD_DIM), lambda h, b: (h, 0, 0))
    out = pl.BlockSpec((SEQ, HEAD_DIM), lambda h, b: (b, h))
    f32_rows = lambda w: pltpu.VMEM((SEQ, w), F32)
    bf16_rows = lambda w: pltpu.VMEM((SEQ, w), BF16)
    return pl.pallas_call(
        _mixers_kernel,
        grid=(HEADS, batch),
        in_specs=[col(0), col(HEADS), col(2 * HEADS)] + [per_head(t) for t in tables]
                 + [col(3 * HEADS), col(4 * HEADS), vec(CONV_W), vec(1), mat, vec(1), mat, vec(1), vec(1)],
        out_specs=[out, out],
        out_shape=[jax.ShapeDtypeStruct((batch * SEQ, D_MODEL), BF16)] * 2,
        scratch_shapes=[f32_rows(128), f32_rows(256), f32_rows(256), bf16_rows(128), bf16_rows(256), bf16_rows(256),
                        f32_rows(HEAD_DIM), f32_rows(HEAD_DIM), f32_rows(HEAD_DIM)]
                       + [f32_rows(HEAD_DIM)] * 4,
        compiler_params=_cparams("parallel", "parallel"),
        name="token_mixers",
    )(proj, proj, proj, *tables, proj, proj, conv_w, conv_b.reshape(1, -1), wa, ba.reshape(1, -1), wx,
      bx.reshape(1, -1), lam.reshape(1, -1))


MG_TM = 1024
MG_TN = 512


def _merge_kernel(ya_ref, yl_ref, ga_ref, gl_ref, wa_ref, wl_ref, o_ref):
    pa = jnp.dot(ya_ref[...], wa_ref[...].astype(BF16), preferred_element_type=F32)
    pl_ = jnp.dot(yl_ref[...], wl_ref[...].astype(BF16), preferred_element_type=F32)
    merged = _sigmoid(ga_ref[...].astype(F32)) * pa + _sigmoid(gl_ref[...].astype(F32)) * pl_
    o_ref[...] = merged.astype(o_ref.dtype)


def _merge(y_attn, y_lru, proj, w_pa, w_pl):
    t = y_attn.shape[0]
    nj = D_MODEL // MG_TN
    act = pl.BlockSpec((MG_TM, D_MODEL), lambda i, j: (i, 0))
    gate = lambda off: pl.BlockSpec((MG_TM, MG_TN), lambda i, j: (i, off * nj + j))
    w = pl.BlockSpec((D_MODEL, MG_TN), lambda i, j: (0, j))
    return pl.pallas_call(
        _merge_kernel,
        grid=(t // MG_TM, nj),
        in_specs=[act, act, gate(5), gate(6), w, w],
        out_specs=pl.BlockSpec((MG_TM, MG_TN), lambda i, j: (i, j)),
        out_shape=jax.ShapeDtypeStruct((t, D_MODEL), BF16),
        compiler_params=_cparams("parallel", "arbitrary"),
        name="gated_merge",
    )(y_attn, y_lru, proj, proj, w_pa, w_pl)


OUT_TN = 1024


def _out_proj_kernel(x_ref, m_ref, w_ref, o_ref):
    y = jnp.dot(m_ref[...], w_ref[...].astype(BF16), preferred_element_type=F32)
    for s in range(HALF):
        o_ref[:, s, :] = x_ref[:, s, :] + y[s * SLAB:(s + 1) * SLAB, :]


def _out_proj(x4, merged, w_out):
    b = x4.shape[0]
    tok = pl.BlockSpec((None, SLAB, HALF, OUT_TN), lambda bi, i, j: (bi, 0, i, j))
    return pl.pallas_call(
        _out_proj_kernel,
        grid=(b, 2, D_MODEL // OUT_TN),
        in_specs=[
            tok,
            pl.BlockSpec((HALF * SLAB, D_MODEL), lambda bi, i, j: (bi * 2 + i, 0)),
            pl.BlockSpec((D_MODEL, OUT_TN), lambda bi, i, j: (0, j)),
        ],
        out_specs=tok,
        out_shape=jax.ShapeDtypeStruct(x4.shape, F32),
        compiler_params=_cparams("parallel", "parallel", "arbitrary"),
        name="out_proj_residual",
    )(x4, merged, w_out)


MLP_TM = 1024
MLP_TF = 512


def _mlp_kernel(h_ref, g_ref, wu_ref, wd_ref, gf_ref, o_ref, hn_ref):
    f = pl.program_id(1)

    @pl.when(f == 0)
    def _():
        h = h_ref[...]
        ms = jnp.mean(h * h, axis=-1, keepdims=True)
        hn_ref[...] = (h * lax.rsqrt(ms + EPS) * g_ref[...]).astype(BF16)
        o_ref[...] = jnp.zeros_like(o_ref)

    up = jnp.dot(hn_ref[...], wu_ref[...], preferred_element_type=F32)
    hid = jnp.square(jnp.maximum(up, 0.0)).astype(BF16)
    o_ref[...] += jnp.dot(hid, wd_ref[...], preferred_element_type=F32)

    @pl.when(f == pl.num_programs(1) - 1)
    def _():
        h2 = h_ref[...] + o_ref[...]
        ms = jnp.mean(h2 * h2, axis=-1, keepdims=True)
        o_ref[...] = h2 * lax.rsqrt(ms + EPS) * gf_ref[...]


def _mlp(h1, g_mlp, w_up, w_down, g_final):
    t = h1.shape[0]
    return pl.pallas_call(
        _mlp_kernel,
        grid=(t // MLP_TM, D_FF // MLP_TF),
        in_specs=[
            pl.BlockSpec((MLP_TM, D_MODEL), lambda i, f: (i, 0)),
            pl.BlockSpec((1, D_MODEL), lambda i, f: (0, 0)),
            pl.BlockSpec((D_MODEL, MLP_TF), lambda i, f: (0, f)),
            pl.BlockSpec((MLP_TF, D_MODEL), lambda i, f: (f, 0)),
            pl.BlockSpec((1, D_MODEL), lambda i, f: (0, 0)),
        ],
        out_specs=pl.BlockSpec((MLP_TM, D_MODEL), lambda i, f: (i, 0)),
        out_shape=jax.ShapeDtypeStruct((t, D_MODEL), F32),
        scratch_shapes=[pltpu.VMEM((MLP_TM, D_MODEL), BF16)],
        compiler_params=_cparams("parallel", "arbitrary"),
        name="mlp_final_norm",
    )(h1, g_mlp.reshape(1, -1), w_up, w_down, g_final.reshape(1, -1))


def kernel(x, norm_mix_g, w_in, conv_w, conv_b, lru_wa, lru_ba, lru_wx, lru_bx, lru_lambda,
           w_proj_attn, w_proj_lru, w_out, norm_mlp_g, w_up, w_down, norm_final_g):
    assert w_in.shape[0] == 1, "single-layer block"
    batch = x.shape[0]
    x4 = x.reshape(batch, SLAB, STRIDE, D_MODEL)
    proj = _in_proj(x4, norm_mix_g[0], w_in[0])
    y_attn, y_lru = _mixers(proj, batch, conv_w[0], conv_b[0], lru_wa[0], lru_ba[0], lru_wx[0], lru_bx[0],
                            lru_lambda[0])
    merged = _merge(y_attn, y_lru, proj, w_proj_attn[0], w_proj_lru[0])
    h1 = _out_proj(x4, merged, w_out[0])
    out = _mlp(h1.reshape(batch * SEQ, D_MODEL), norm_mlp_g[0], w_up[0].astype(BF16), w_down[0].astype(BF16),
               norm_final_g)
    return out.reshape(batch, SEQ, D_MODEL)
```

```python
import jax
import jax.numpy as jnp
import numpy as np
from jax import lax
from jax.experimental import pallas as pl
from jax.experimental.pallas import tpu as pltpu

D_MODEL = 2048
SEQ = 2048
HEADS = 16
HEAD_DIM = 128
D_FF = 4 * D_MODEL
IN_COLS = 7 * D_MODEL
CONV_W = 4
LRU_C = 8.0
EPS = 1e-6

STRIDE = 16
SLAB = SEQ // STRIDE
HALF = STRIDE // 2
MASKED = -1e30
LOG2E = 1.4426950408889634
Q_SCALE = HEAD_DIM ** -0.5 * LOG2E

V7X_VMEM_BYTES = 64 * 1024 * 1024
VMEM_LIMIT = V7X_VMEM_BYTES - 8 * 1024 * 1024

BF16 = jnp.bfloat16
F32 = jnp.float32


def _cparams(*sem):
    return pltpu.CompilerParams(dimension_semantics=sem, vmem_limit_bytes=VMEM_LIMIT)


def _sigmoid(x):
    return 1.0 / (1.0 + jnp.exp2(x * (-LOG2E)))


def _gelu_tanh(x):
    c = 0.7978845608028654
    return x * (0.5 + 0.5 * jnp.tanh(x * (c + (c * 0.044715) * (x * x))))


NORM_TM = HALF * SLAB
IN_TM = SEQ
IN_TN = 1024
LRU_GATE_SPLIT = 4


def _norm_kernel(x_ref, g_ref, o_ref, xp_ref):
    for s in range(HALF):
        xp_ref[s * SLAB:(s + 1) * SLAB, :] = x_ref[:, s, :]
    xs = xp_ref[...]
    ms = jnp.mean(xs * xs, axis=-1, keepdims=True)
    o_ref[...] = (xs * lax.rsqrt(ms + EPS) * g_ref[...]).astype(o_ref.dtype)


def _norm_permute(x4, g):
    b = x4.shape[0]
    return pl.pallas_call(
        _norm_kernel,
        grid=(b, 2),
        in_specs=[
            pl.BlockSpec((None, SLAB, HALF, D_MODEL), lambda bi, i: (bi, 0, i, 0)),
            pl.BlockSpec((1, D_MODEL), lambda bi, i: (0, 0)),
        ],
        out_specs=pl.BlockSpec((NORM_TM, D_MODEL), lambda bi, i: (bi * 2 + i, 0)),
        out_shape=jax.ShapeDtypeStruct((b * SEQ, D_MODEL), BF16),
        scratch_shapes=[pltpu.VMEM((NORM_TM, D_MODEL), F32)],
        compiler_params=_cparams("parallel", "parallel"),
        name="norm_permute",
    )(x4, g.reshape(1, D_MODEL))


def _in_proj_kernel(xn_ref, w_ref, o_ref):
    j = pl.program_id(1)
    tiles = D_MODEL // IN_TN
    is_q = j < tiles
    is_lru_gate = jnp.logical_and(j >= LRU_GATE_SPLIT * tiles, j < (LRU_GATE_SPLIT + 1) * tiles)

    def project(epilogue):
        y = jnp.dot(xn_ref[...], w_ref[...].astype(BF16), preferred_element_type=F32)
        o_ref[...] = epilogue(y).astype(o_ref.dtype)

    pl.when(is_q)(lambda: project(lambda y: y * Q_SCALE))
    pl.when(is_lru_gate)(lambda: project(_gelu_tanh))
    pl.when(jnp.logical_not(jnp.logical_or(is_q, is_lru_gate)))(lambda: project(lambda y: y))


def _in_proj(xn, w):
    t = xn.shape[0]
    return pl.pallas_call(
        _in_proj_kernel,
        grid=(t // IN_TM, IN_COLS // IN_TN),
        in_specs=[
            pl.BlockSpec((IN_TM, D_MODEL), lambda i, j: (i, 0)),
            pl.BlockSpec((D_MODEL, IN_TN), lambda i, j: (0, j)),
        ],
        out_specs=pl.BlockSpec((IN_TM, IN_TN), lambda i, j: (i, j)),
        out_shape=jax.ShapeDtypeStruct((t, IN_COLS), BF16),
        compiler_params=_cparams("parallel", "arbitrary"),
        name="in_proj",
    )(xn, w)


def _bias_tables():
    slopes = 2.0 ** (-8.0 * np.arange(1, HEADS + 1, dtype=np.float64) / HEADS)

    def table(tq, tk, span):
        dist = (tq[:, None] - tk[None, :]).astype(np.float64)
        ok = (dist >= 0) & (dist <= span)
        bias = -slopes[:, None, None] * dist[None] * LOG2E
        return np.where(ok[None], bias, MASKED).astype(np.float32)

    u = np.arange(SLAB)
    b16 = table(16 * u, 16 * u, 16 * 128)
    m4, ul = np.meshgrid(np.arange(4), np.arange(32), indexing="ij")
    m4k, ulk = np.meshgrid(np.arange(4), np.arange(64), indexing="ij")
    tk4 = (16 * ulk + 4 * m4k).reshape(-1)
    b4_first = table((16 * ul + 4 * m4).reshape(-1), tk4, 4 * 128)
    b4 = table((16 * (32 + ul) + 4 * m4).reshape(-1), tk4, 4 * 128)
    r1, ul1 = np.meshgrid(np.arange(16), np.arange(D1_Q), indexing="ij")
    r1k, ul1k = np.meshgrid(np.arange(16), np.arange(2 * D1_Q), indexing="ij")
    tk1 = (16 * ul1k + r1k).reshape(-1)
    b1_first = table((16 * ul1 + r1).reshape(-1), tk1, 128)
    b1 = table((16 * (D1_Q + ul1) + r1).reshape(-1), tk1, 128)
    return b16, b4_first, b4, b1_first, b1


def _rows(ref, starts, size):
    return jnp.concatenate([ref[s:s + size, :] for s in starts], axis=0)


def _scatter(ref, starts, size, val):
    for i, st in enumerate(starts):
        ref[st:st + size, :] = val[i * size:(i + 1) * size, :]


def _d4_blocks():
    for r4 in range(4):
        slabs = [(r4 + 4 * m) * SLAB for m in range(4)]
        for n in range(4):
            yield [b + 32 * n for b in slabs], [b + 32 * max(n - 1, 0) for b in slabs], n == 0


D1_Q = 8


def _d1_blocks():
    for n in range(SLAB // D1_Q):
        yield ([r * SLAB + D1_Q * n for r in range(STRIDE)],
               [r * SLAB + D1_Q * max(n - 1, 0) for r in range(STRIDE)], n == 0)


def _attn_body(q_ref, k_ref, v_ref, b16_ref, b4f_ref, b4_ref, b1f_ref, b1_ref, o_ref,
               s16, s4, s1, p16, p4, p1, a16, a4, linv):
    def qk(q, k):
        return lax.dot_general(q, k, (((1,), (1,)), ((), ())), preferred_element_type=F32)

    for r in range(STRIDE):
        rows = slice(r * SLAB, (r + 1) * SLAB)
        s16[rows, :] = qk(q_ref[rows, :], k_ref[rows, :]) + b16_ref[...]
    for q_st, k_st, first in _d4_blocks():
        s = qk(_rows(q_ref, q_st, 32), _rows(k_ref, k_st, 64)) + (b4f_ref if first else b4_ref)[...]
        _scatter(s4, q_st, 32, s)
    for q_st, k_st, first in _d1_blocks():
        s = qk(_rows(q_ref, q_st, D1_Q), _rows(k_ref, k_st, 2 * D1_Q)) + (b1f_ref if first else b1_ref)[...]
        _scatter(s1, q_st, D1_Q, s)

    for r in range(STRIDE):
        rows = slice(r * SLAB, (r + 1) * SLAB)
        x16, x4, x1 = s16[rows, :], s4[rows, :], s1[rows, :]
        m = jnp.maximum(jnp.maximum(jnp.max(x16, axis=-1, keepdims=True), jnp.max(x4, axis=-1, keepdims=True)),
                        jnp.max(x1, axis=-1, keepdims=True))
        e16 = jnp.exp2(x16 - m)
        e4 = jnp.exp2(x4 - m)
        e1 = jnp.exp2(x1 - m)
        l = (jnp.sum(e16, axis=-1, keepdims=True) + jnp.sum(e4, axis=-1, keepdims=True)
             + jnp.sum(e1, axis=-1, keepdims=True))
        p16[rows, :] = e16.astype(BF16)
        p4[rows, :] = e4.astype(BF16)
        p1[rows, :] = e1.astype(BF16)
        linv[rows, :] = jnp.broadcast_to(1.0 / l, (SLAB, HEAD_DIM))

    for r in range(STRIDE):
        rows = slice(r * SLAB, (r + 1) * SLAB)
        a16[rows, :] = jnp.dot(p16[rows, :], v_ref[rows, :], preferred_element_type=F32)
    for q_st, k_st, _ in _d4_blocks():
        _scatter(a4, q_st, 32, jnp.dot(_rows(p4, q_st, 32), _rows(v_ref, k_st, 64), preferred_element_type=F32))
    for q_st, k_st, _ in _d1_blocks():
        acc = jnp.dot(_rows(p1, q_st, D1_Q), _rows(v_ref, k_st, 2 * D1_Q), preferred_element_type=F32)
        acc = acc + _rows(a16, q_st, D1_Q) + _rows(a4, q_st, D1_Q)
        _scatter(o_ref, q_st, D1_Q, (acc * _rows(linv, q_st, D1_Q)).astype(o_ref.dtype))


def _shift_rows(x, k, fill):
    n = x.shape[0]
    if k % 8 == 0:
        return jnp.concatenate([jnp.full((k, x.shape[1]), fill, x.dtype), x[:n - k, :]], axis=0)
    rolled = pltpu.roll(x, k, axis=0)
    row = lax.broadcasted_iota(jnp.int32, x.shape, 0)
    return jnp.where(row < k, fill, rolled)


def _lru_body(xr_ref, gg_ref, cw_ref, cb_ref, wa_ref, ba_ref, wx_ref, bx_ref, lam_ref,
              o_ref, x32_ref, xc_ref, a_ref, b_ref):
    cw = cw_ref[...]
    cb = cb_ref[...]
    x32_ref[...] = xr_ref[...].astype(F32)

    def tap(r):
        x = x32_ref[(r % STRIDE) * SLAB:(r % STRIDE + 1) * SLAB, :]
        return x if r >= 0 else _shift_rows(x, 1, 0.0)

    for r in range(STRIDE):
        xc = cb + cw[3:4, :] * tap(r)
        for j in range(1, CONV_W):
            xc = xc + cw[3 - j:4 - j, :] * tap(r - j)
        xc_ref[r * SLAB:(r + 1) * SLAB, :] = xc

    w_gates = jnp.concatenate([wa_ref[...], wx_ref[...]], axis=1).astype(BF16)
    gates = jnp.dot(xc_ref[...].astype(BF16), w_gates, preferred_element_type=F32)
    rg = _sigmoid(gates[:, :HEAD_DIM] + ba_ref[...])
    ig = _sigmoid(gates[:, HEAD_DIM:] + bx_ref[...])
    z = -lam_ref[...]
    softplus = jnp.maximum(z, 0.0) + jnp.log1p(jnp.exp(-jnp.abs(z)))
    log_a = rg * ((-LRU_C) * softplus)
    a = jnp.exp(log_a)
    a_ref[...] = a
    m2 = -jnp.tanh(log_a) * (a * a + 1.0)
    mult = jnp.where(m2 == 0.0, 0.0, m2 * lax.rsqrt(m2))
    b_ref[...] = mult * (ig * xc_ref[...])

    for r in range(1, STRIDE):
        prev = slice((r - 1) * SLAB, r * SLAB)
        cur = slice(r * SLAB, (r + 1) * SLAB)
        a_cur = a_ref[cur, :]
        b_ref[cur, :] = a_cur * b_ref[prev, :] + b_ref[cur, :]
        a_ref[cur, :] = a_cur * a_ref[prev, :]

    last = slice((STRIDE - 1) * SLAB, STRIDE * SLAB)
    ta = a_ref[last, :]
    tb = b_ref[last, :]
    k = 1
    while k < SLAB:
        tb = ta * _shift_rows(tb, k, 0.0) + tb
        ta = ta * _shift_rows(ta, k, 1.0)
        k *= 2
    carry = _shift_rows(tb, 1, 0.0)

    for r in range(STRIDE):
        cur = slice(r * SLAB, (r + 1) * SLAB)
        h = b_ref[cur, :] + a_ref[cur, :] * carry
        o_ref[cur, :] = (h * gg_ref[cur, :].astype(F32)).astype(o_ref.dtype)


N_ATTN_IN = 8
N_LRU_IN = 9


def _mixers_kernel(*refs):
    attn_in = refs[:N_ATTN_IN]
    lru_in = refs[N_ATTN_IN:N_ATTN_IN + N_LRU_IN]
    o_attn, o_lru = refs[N_ATTN_IN + N_LRU_IN:N_ATTN_IN + N_LRU_IN + 2]
    scratch = refs[N_ATTN_IN + N_LRU_IN + 2:]
    _attn_body(*attn_in, o_attn, *scratch[:9])
    _lru_body(*lru_in, o_lru, *scratch[9:])


def _mixers(proj, batch, conv_w, conv_b, wa, ba, wx, bx, lam):
    tables = [jnp.asarray(t) for t in _bias_tables()]
    per_head = lambda t: pl.BlockSpec((None,) + t.shape[1:], lambda h, b: (h, 0, 0))
    col = lambda off: pl.BlockSpec((SEQ, HEAD_DIM), lambda h, b: (b, off + h))
    vec = lambda rows: pl.BlockSpec((rows, HEAD_DIM), lambda h, b: (0, h))
    mat = pl.BlockSpec((None, HEAD_DIM, HEAD_DIM), lambda h, b: (h, 0, 0))
    out = pl.BlockSpec((SEQ, HEAD_DIM), lambda h, b: (b, h))
    f32_rows = lambda w: pltpu.VMEM((SEQ, w), F32)
    bf16_rows = lambda w: pltpu.VMEM((SEQ, w), BF16)
    return pl.pallas_call(
        _mixers_kernel,
        grid=(HEADS, batch),
        in_specs=[col(0), col(HEADS), col(2 * HEADS)] + [per_head(t) for t in tables]
                 + [col(3 * HEADS), col(4 * HEADS), vec(CONV_W), vec(1), mat, vec(1), mat, vec(1), vec(1)],
        out_specs=[out, out],
        out_shape=[jax.ShapeDtypeStruct((batch * SEQ, D_MODEL), BF16)] * 2,
        scratch_shapes=[f32_rows(128), f32_rows(256), f32_rows(256), bf16_rows(128), bf16_rows(256), bf16_rows(256),
                        f32_rows(HEAD_DIM), f32_rows(HEAD_DIM), f32_rows(HEAD_DIM)]
                       + [f32_rows(HEAD_DIM)] * 4,
        compiler_params=_cparams("parallel", "parallel"),
        name="token_mixers",
    )(proj, proj, proj, *tables, proj, proj, conv_w, conv_b.reshape(1, -1), wa, ba.reshape(1, -1), wx,
      bx.reshape(1, -1), lam.reshape(1, -1))


MG_TM = 1024
MG_TN = 1024


def _merge_kernel(ya_ref, yl_ref, ga_ref, gl_ref, wa_ref, wl_ref, o_ref):
    pa = jnp.dot(ya_ref[...], wa_ref[...], preferred_element_type=F32)
    pl_ = jnp.dot(yl_ref[...], wl_ref[...], preferred_element_type=F32)
    merged = _sigmoid(ga_ref[...].astype(F32)) * pa + _sigmoid(gl_ref[...].astype(F32)) * pl_
    o_ref[...] = merged.astype(o_ref.dtype)


def _merge(y_attn, y_lru, proj, w_pa, w_pl):
    t = y_attn.shape[0]
    nj = D_MODEL // MG_TN
    act = pl.BlockSpec((MG_TM, D_MODEL), lambda i, j: (i, 0))
    gate = lambda off: pl.BlockSpec((MG_TM, MG_TN), lambda i, j: (i, off * nj + j))
    w = pl.BlockSpec((D_MODEL, MG_TN), lambda i, j: (0, j))
    return pl.pallas_call(
        _merge_kernel,
        grid=(t // MG_TM, nj),
        in_specs=[act, act, gate(5), gate(6), w, w],
        out_specs=pl.BlockSpec((MG_TM, MG_TN), lambda i, j: (i, j)),
        out_shape=jax.ShapeDtypeStruct((t, D_MODEL), BF16),
        compiler_params=_cparams("parallel", "arbitrary"),
        name="gated_merge",
    )(y_attn, y_lru, proj, proj, w_pa, w_pl)


OUT_TN = 512


def _out_proj_kernel(x_ref, m_ref, w_ref, o_ref):
    y = jnp.dot(m_ref[...], w_ref[...].astype(BF16), preferred_element_type=F32)
    for s in range(STRIDE):
        o_ref[:, s, :] = x_ref[:, s, :] + y[s * SLAB:(s + 1) * SLAB, :]


def _out_proj(x4, merged, w_out):
    b = x4.shape[0]
    tok = pl.BlockSpec((None, SLAB, STRIDE, OUT_TN), lambda bi, j: (bi, 0, 0, j))
    return pl.pallas_call(
        _out_proj_kernel,
        grid=(b, D_MODEL // OUT_TN),
        in_specs=[
            tok,
            pl.BlockSpec((SEQ, D_MODEL), lambda bi, j: (bi, 0)),
            pl.BlockSpec((D_MODEL, OUT_TN), lambda bi, j: (0, j)),
        ],
        out_specs=tok,
        out_shape=jax.ShapeDtypeStruct(x4.shape, F32),
        compiler_params=_cparams("parallel", "arbitrary"),
        name="out_proj_residual",
    )(x4, merged, w_out)


MLP_TM = 1024
MLP_TF = 1024
MLP_VMEM_LIMIT = V7X_VMEM_BYTES - 2 * 1024 * 1024


def _mlp_kernel(h_ref, g_ref, wu_ref, wd_ref, gf_ref, o_ref, hn_ref):
    f = pl.program_id(1)

    @pl.when(f == 0)
    def _():
        h = h_ref[...]
        ms = jnp.mean(h * h, axis=-1, keepdims=True)
        hn_ref[...] = (h * lax.rsqrt(ms + EPS) * g_ref[...]).astype(BF16)

    def accumulate(first):
        up = jnp.dot(hn_ref[...], wu_ref[...], preferred_element_type=F32)
        hid = jnp.square(jnp.maximum(up, 0.0)).astype(BF16)
        part = jnp.dot(hid, wd_ref[...], preferred_element_type=F32)
        if first:
            o_ref[...] = part
        else:
            o_ref[...] += part

    pl.when(f == 0)(lambda: accumulate(True))
    pl.when(f > 0)(lambda: accumulate(False))

    @pl.when(f == pl.num_programs(1) - 1)
    def _():
        h2 = h_ref[...] + o_ref[...]
        ms = jnp.mean(h2 * h2, axis=-1, keepdims=True)
        o_ref[...] = h2 * lax.rsqrt(ms + EPS) * gf_ref[...]


def _mlp(h1, g_mlp, w_up, w_down, g_final):
    t = h1.shape[0]
    return pl.pallas_call(
        _mlp_kernel,
        grid=(t // MLP_TM, D_FF // MLP_TF),
        in_specs=[
            pl.BlockSpec((MLP_TM, D_MODEL), lambda i, f: (i, 0)),
            pl.BlockSpec((1, D_MODEL), lambda i, f: (0, 0)),
            pl.BlockSpec((D_MODEL, MLP_TF), lambda i, f: (0, f)),
            pl.BlockSpec((MLP_TF, D_MODEL), lambda i, f: (f, 0)),
            pl.BlockSpec((1, D_MODEL), lambda i, f: (0, 0)),
        ],
        out_specs=pl.BlockSpec((MLP_TM, D_MODEL), lambda i, f: (i, 0)),
        out_shape=jax.ShapeDtypeStruct((t, D_MODEL), F32),
        scratch_shapes=[pltpu.VMEM((MLP_TM, D_MODEL), BF16)],
        compiler_params=pltpu.CompilerParams(dimension_semantics=("parallel", "arbitrary"),
                                             vmem_limit_bytes=MLP_VMEM_LIMIT),
        name="mlp_final_norm",
    )(h1, g_mlp.reshape(1, -1), w_up, w_down, g_final.reshape(1, -1))


def kernel(x, norm_mix_g, w_in, conv_w, conv_b, lru_wa, lru_ba, lru_wx, lru_bx, lru_lambda,
           w_proj_attn, w_proj_lru, w_out, norm_mlp_g, w_up, w_down, norm_final_g):
    assert w_in.shape[0] == 1, "single-layer block"
    batch = x.shape[0]
    x4 = x.reshape(batch, SLAB, STRIDE, D_MODEL)
    proj = _in_proj(_norm_permute(x4, norm_mix_g[0]), w_in[0])
    y_attn, y_lru = _mixers(proj, batch, conv_w[0], conv_b[0], lru_wa[0], lru_ba[0], lru_wx[0], lru_bx[0],
                            lru_lambda[0])
    merged = _merge(y_attn, y_lru, proj, w_proj_attn[0].astype(BF16), w_proj_lru[0].astype(BF16))
    h1 = _out_proj(x4, merged, w_out[0])
    out = _mlp(h1.reshape(batch * SEQ, D_MODEL), norm_mlp_g[0], w_up[0].astype(BF16), w_down[0].astype(BF16),
               norm_final_g)
    return out.reshape(batch, SEQ, D_MODEL)
```

```python
import jax
import jax.numpy as jnp
import numpy as np
from jax import lax
from jax.experimental import pallas as pl
from jax.experimental.pallas import tpu as pltpu

D_MODEL = 2048
SEQ = 2048
HEADS = 16
HEAD_DIM = 128
D_FF = 4 * D_MODEL
IN_COLS = 7 * D_MODEL
CONV_W = 4
LRU_C = 8.0
EPS = 1e-6

STRIDE = 16
SLAB = SEQ // STRIDE
HALF = STRIDE // 2
MASKED = -1e30
LOG2E = 1.4426950408889634
Q_SCALE = HEAD_DIM ** -0.5 * LOG2E

V7X_VMEM_BYTES = 64 * 1024 * 1024
VMEM_LIMIT = V7X_VMEM_BYTES - 8 * 1024 * 1024

BF16 = jnp.bfloat16
F32 = jnp.float32


def _cparams(*sem):
    return pltpu.CompilerParams(dimension_semantics=sem, vmem_limit_bytes=VMEM_LIMIT)


def _sigmoid(x):
    return 1.0 / (1.0 + jnp.exp2(x * (-LOG2E)))


def _gelu_tanh(x):
    c = 0.7978845608028654
    return x * (0.5 + 0.5 * jnp.tanh(x * (c + (c * 0.044715) * (x * x))))


NORM_TM = HALF * SLAB
IN_TM = SEQ
IN_TN = 1024
LRU_GATE_SPLIT = 4


def _norm_kernel(x_ref, g_ref, o_ref, xp_ref):
    for s in range(HALF):
        xp_ref[s * SLAB:(s + 1) * SLAB, :] = x_ref[:, s, :]
    xs = xp_ref[...]
    ms = jnp.mean(xs * xs, axis=-1, keepdims=True)
    o_ref[...] = (xs * lax.rsqrt(ms + EPS) * g_ref[...]).astype(o_ref.dtype)


def _norm_permute(x4, g):
    b = x4.shape[0]
    return pl.pallas_call(
        _norm_kernel,
        grid=(b, 2),
        in_specs=[
            pl.BlockSpec((None, SLAB, HALF, D_MODEL), lambda bi, i: (bi, 0, i, 0)),
            pl.BlockSpec((1, D_MODEL), lambda bi, i: (0, 0)),
        ],
        out_specs=pl.BlockSpec((NORM_TM, D_MODEL), lambda bi, i: (bi * 2 + i, 0)),
        out_shape=jax.ShapeDtypeStruct((b * SEQ, D_MODEL), BF16),
        scratch_shapes=[pltpu.VMEM((NORM_TM, D_MODEL), F32)],
        compiler_params=_cparams("parallel", "parallel"),
        name="norm_permute",
    )(x4, g.reshape(1, D_MODEL))


def _in_proj_kernel(xn_ref, w_ref, o_ref):
    j = pl.program_id(1)
    tiles = D_MODEL // IN_TN
    is_q = j < tiles
    is_lru_gate = jnp.logical_and(j >= LRU_GATE_SPLIT * tiles, j < (LRU_GATE_SPLIT + 1) * tiles)

    def project(epilogue):
        y = jnp.dot(xn_ref[...], w_ref[...].astype(BF16), preferred_element_type=F32)
        o_ref[...] = epilogue(y).astype(o_ref.dtype)

    pl.when(is_q)(lambda: project(lambda y: y * Q_SCALE))
    pl.when(is_lru_gate)(lambda: project(_gelu_tanh))
    pl.when(jnp.logical_not(jnp.logical_or(is_q, is_lru_gate)))(lambda: project(lambda y: y))


def _in_proj(xn, w):
    t = xn.shape[0]
    return pl.pallas_call(
        _in_proj_kernel,
        grid=(t // IN_TM, IN_COLS // IN_TN),
        in_specs=[
            pl.BlockSpec((IN_TM, D_MODEL), lambda i, j: (i, 0)),
            pl.BlockSpec((D_MODEL, IN_TN), lambda i, j: (0, j)),
        ],
        out_specs=pl.BlockSpec((IN_TM, IN_TN), lambda i, j: (i, j)),
        out_shape=jax.ShapeDtypeStruct((t, IN_COLS), BF16),
        compiler_params=_cparams("parallel", "arbitrary"),
        name="in_proj",
    )(xn, w)


def _bias_tables():
    slopes = 2.0 ** (-8.0 * np.arange(1, HEADS + 1, dtype=np.float64) / HEADS)

    def table(tq, tk, span):
        dist = (tq[:, None] - tk[None, :]).astype(np.float64)
        ok = (dist >= 0) & (dist <= span)
        bias = -slopes[:, None, None] * dist[None] * LOG2E
        return np.where(ok[None], bias, MASKED).astype(np.float32)

    u = np.arange(SLAB)
    b16 = table(16 * u, 16 * u, 16 * 128)
    m4, ul = np.meshgrid(np.arange(4), np.arange(32), indexing="ij")
    m4k, ulk = np.meshgrid(np.arange(4), np.arange(64), indexing="ij")
    tk4 = (16 * ulk + 4 * m4k).reshape(-1)
    b4_first = table((16 * ul + 4 * m4).reshape(-1), tk4, 4 * 128)
    b4 = table((16 * (32 + ul) + 4 * m4).reshape(-1), tk4, 4 * 128)
    r1, ul1 = np.meshgrid(np.arange(16), np.arange(D1_Q), indexing="ij")
    r1k, ul1k = np.meshgrid(np.arange(16), np.arange(2 * D1_Q), indexing="ij")
    tk1 = (16 * ul1k + r1k).reshape(-1)
    b1_first = table((16 * ul1 + r1).reshape(-1), tk1, 128)
    b1 = table((16 * (D1_Q + ul1) + r1).reshape(-1), tk1, 128)
    return b16, b4_first, b4, b1_first, b1


def _rows(ref, starts, size):
    return jnp.concatenate([ref[s:s + size, :] for s in starts], axis=0)


def _scatter(ref, starts, size, val):
    for i, st in enumerate(starts):
        ref[st:st + size, :] = val[i * size:(i + 1) * size, :]


def _d4_blocks():
    for r4 in range(4):
        slabs = [(r4 + 4 * m) * SLAB for m in range(4)]
        for n in range(4):
            yield [b + 32 * n for b in slabs], [b + 32 * max(n - 1, 0) for b in slabs], n == 0


D1_Q = 8


def _d1_blocks():
    for n in range(SLAB // D1_Q):
        yield ([r * SLAB + D1_Q * n for r in range(STRIDE)],
               [r * SLAB + D1_Q * max(n - 1, 0) for r in range(STRIDE)], n == 0)


def _attn_body(q_ref, k_ref, v_ref, b16_ref, b4f_ref, b4_ref, b1f_ref, b1_ref, o_ref,
               s16, s4, s1, p16, p4, p1, a16, a4, linv):
    def qk(q, k):
        return lax.dot_general(q, k, (((1,), (1,)), ((), ())), preferred_element_type=F32)

    for r in range(STRIDE):
        rows = slice(r * SLAB, (r + 1) * SLAB)
        s16[rows, :] = qk(q_ref[rows, :], k_ref[rows, :]) + b16_ref[...]
        yield
    for q_st, k_st, first in _d4_blocks():
        s = qk(_rows(q_ref, q_st, 32), _rows(k_ref, k_st, 64)) + (b4f_ref if first else b4_ref)[...]
        _scatter(s4, q_st, 32, s)
        yield
    for q_st, k_st, first in _d1_blocks():
        s = qk(_rows(q_ref, q_st, D1_Q), _rows(k_ref, k_st, 2 * D1_Q)) + (b1f_ref if first else b1_ref)[...]
        _scatter(s1, q_st, D1_Q, s)
        yield

    for r in range(STRIDE):
        rows = slice(r * SLAB, (r + 1) * SLAB)
        x16, x4, x1 = s16[rows, :], s4[rows, :], s1[rows, :]
        m = jnp.maximum(jnp.maximum(jnp.max(x16, axis=-1, keepdims=True), jnp.max(x4, axis=-1, keepdims=True)),
                        jnp.max(x1, axis=-1, keepdims=True))
        e16 = jnp.exp2(x16 - m)
        e4 = jnp.exp2(x4 - m)
        e1 = jnp.exp2(x1 - m)
        l = (jnp.sum(e16, axis=-1, keepdims=True) + jnp.sum(e4, axis=-1, keepdims=True)
             + jnp.sum(e1, axis=-1, keepdims=True))
        p16[rows, :] = e16.astype(BF16)
        p4[rows, :] = e4.astype(BF16)
        p1[rows, :] = e1.astype(BF16)
        linv[rows, :] = jnp.broadcast_to(1.0 / l, (SLAB, HEAD_DIM))
        yield

    for r in range(STRIDE):
        rows = slice(r * SLAB, (r + 1) * SLAB)
        a16[rows, :] = jnp.dot(p16[rows, :], v_ref[rows, :], preferred_element_type=F32)
        yield
    for q_st, k_st, _ in _d4_blocks():
        _scatter(a4, q_st, 32, jnp.dot(_rows(p4, q_st, 32), _rows(v_ref, k_st, 64), preferred_element_type=F32))
        yield
    for q_st, k_st, _ in _d1_blocks():
        acc = jnp.dot(_rows(p1, q_st, D1_Q), _rows(v_ref, k_st, 2 * D1_Q), preferred_element_type=F32)
        acc = acc + _rows(a16, q_st, D1_Q) + _rows(a4, q_st, D1_Q)
        _scatter(o_ref, q_st, D1_Q, (acc * _rows(linv, q_st, D1_Q)).astype(o_ref.dtype))
        yield


def _shift_rows(x, k, fill):
    n = x.shape[0]
    if k % 8 == 0:
        return jnp.concatenate([jnp.full((k, x.shape[1]), fill, x.dtype), x[:n - k, :]], axis=0)
    rolled = pltpu.roll(x, k, axis=0)
    row = lax.broadcasted_iota(jnp.int32, x.shape, 0)
    return jnp.where(row < k, fill, rolled)


def _lru_body(xr_ref, gg_ref, cw_ref, cb_ref, wa_ref, ba_ref, wx_ref, bx_ref, lam_ref,
              o_ref, x32_ref, a_ref, b_ref):
    cw = cw_ref[...]
    cb = cb_ref[...]
    x32_ref[...] = xr_ref[...].astype(F32)

    def tap(r):
        x = x32_ref[(r % STRIDE) * SLAB:(r % STRIDE + 1) * SLAB, :]
        return x if r >= 0 else _shift_rows(x, 1, 0.0)

    w_gates = jnp.concatenate([wa_ref[...], wx_ref[...]], axis=1).astype(BF16)
    z = -lam_ref[...]
    softplus = jnp.maximum(z, 0.0) + jnp.log1p(jnp.exp(-jnp.abs(z)))
    c8 = (-LRU_C) * softplus

    ta = tb = None
    for r in range(STRIDE):
        cur = slice(r * SLAB, (r + 1) * SLAB)
        xc = cb + cw[3:4, :] * tap(r)
        for j in range(1, CONV_W):
            xc = xc + cw[3 - j:4 - j, :] * tap(r - j)
        gates = jnp.dot(xc.astype(BF16), w_gates, preferred_element_type=F32)
        rg = _sigmoid(gates[:, :HEAD_DIM] + ba_ref[...])
        ig = _sigmoid(gates[:, HEAD_DIM:] + bx_ref[...])
        log_a = rg * c8
        a = jnp.exp(log_a)
        m2 = -jnp.tanh(log_a) * (a * a + 1.0)
        mult = jnp.where(m2 == 0.0, 0.0, m2 * lax.rsqrt(m2))
        b = mult * (ig * xc)
        if r > 0:
            b = a * tb + b
            a = a * ta
        a_ref[cur, :] = a
        b_ref[cur, :] = b
        ta, tb = a, b
        yield

    k = 1
    while k < SLAB:
        tb = ta * _shift_rows(tb, k, 0.0) + tb
        ta = ta * _shift_rows(ta, k, 1.0)
        k *= 2
    carry = _shift_rows(tb, 1, 0.0)
    yield

    for r in range(STRIDE):
        cur = slice(r * SLAB, (r + 1) * SLAB)
        h = b_ref[cur, :] + a_ref[cur, :] * carry
        o_ref[cur, :] = (h * gg_ref[cur, :].astype(F32)).astype(o_ref.dtype)
        yield


N_ATTN_IN = 8
N_LRU_IN = 9
N_ATTN_SCRATCH = 9
ATTN_ITEMS = 2 * (STRIDE + 16 + SLAB // D1_Q) + STRIDE
LRU_ITEMS = 2 * STRIDE + 1


def _mixers_kernel(*refs):
    attn_in = refs[:N_ATTN_IN]
    lru_in = refs[N_ATTN_IN:N_ATTN_IN + N_LRU_IN]
    o_attn, o_lru = refs[N_ATTN_IN + N_LRU_IN:N_ATTN_IN + N_LRU_IN + 2]
    scratch = refs[N_ATTN_IN + N_LRU_IN + 2:]
    attn = _attn_body(*attn_in, o_attn, *scratch[:N_ATTN_SCRATCH])
    lru = _lru_body(*lru_in, o_lru, *scratch[N_ATTN_SCRATCH:])
    emitted = 0
    for i in range(ATTN_ITEMS):
        next(attn)
        while emitted * ATTN_ITEMS < (i + 1) * LRU_ITEMS:
            next(lru)
            emitted += 1
    assert next(attn, None) is None and next(lru, None) is None


def _mixers(proj, batch, conv_w, conv_b, wa, ba, wx, bx, lam):
    tables = [jnp.asarray(t) for t in _bias_tables()]
    per_head = lambda t: pl.BlockSpec((None,) + t.shape[1:], lambda h, b: (h, 0, 0))
    col = lambda off: pl.BlockSpec((SEQ, HEAD_DIM), lambda h, b: (b, off + h))
    vec = lambda rows: pl.BlockSpec((rows, HEAD_DIM), lambda h, b: (0, h))
    mat = pl.BlockSpec((None, HEAD_DIM, HEAD_DIM), lambda h, b: (h, 0, 0))
    out = pl.BlockSpec((SEQ, HEAD_DIM), lambda h, b: (b, h))
    f32_rows = lambda w: pltpu.VMEM((SEQ, w), F32)
    bf16_rows = lambda w: pltpu.VMEM((SEQ, w), BF16)
    return pl.pallas_call(
        _mixers_kernel,
        grid=(HEADS, batch),
        in_specs=[col(0), col(HEADS), col(2 * HEADS)] + [per_head(t) for t in tables]
                 + [col(3 * HEADS), col(4 * HEADS), vec(CONV_W), vec(1), mat, vec(1), mat, vec(1), vec(1)],
        out_specs=[out, out],
        out_shape=[jax.ShapeDtypeStruct((batch * SEQ, D_MODEL), BF16)] * 2,
        scratch_shapes=[f32_rows(128), f32_rows(256), f32_rows(256), bf16_rows(128), bf16_rows(256), bf16_rows(256),
                        f32_rows(HEAD_DIM), f32_rows(HEAD_DIM), f32_rows(HEAD_DIM)]
                       + [f32_rows(HEAD_DIM)] * 3,
        compiler_params=_cparams("parallel", "parallel"),
        name="token_mixers",
    )(proj, proj, proj, *tables, proj, proj, conv_w, conv_b.reshape(1, -1), wa, ba.reshape(1, -1), wx,
      bx.reshape(1, -1), lam.reshape(1, -1))


MG_TM = 1024
MG_TN = 1024


def _merge_kernel(ya_ref, yl_ref, ga_ref, gl_ref, wa_ref, wl_ref, o_ref):
    pa = jnp.dot(ya_ref[...], wa_ref[...], preferred_element_type=F32)
    pl_ = jnp.dot(yl_ref[...], wl_ref[...], preferred_element_type=F32)
    merged = _sigmoid(ga_ref[...].astype(F32)) * pa + _sigmoid(gl_ref[...].astype(F32)) * pl_
    o_ref[...] = merged.astype(o_ref.dtype)


def _merge(y_attn, y_lru, proj, w_pa, w_pl):
    t = y_attn.shape[0]
    nj = D_MODEL // MG_TN
    act = pl.BlockSpec((MG_TM, D_MODEL), lambda i, j: (i, 0))
    gate = lambda off: pl.BlockSpec((MG_TM, MG_TN), lambda i, j: (i, off * nj + j))
    w = pl.BlockSpec((D_MODEL, MG_TN), lambda i, j: (0, j))
    return pl.pallas_call(
        _merge_kernel,
        grid=(t // MG_TM, nj),
        in_specs=[act, act, gate(5), gate(6), w, w],
        out_specs=pl.BlockSpec((MG_TM, MG_TN), lambda i, j: (i, j)),
        out_shape=jax.ShapeDtypeStruct((t, D_MODEL), BF16),
        compiler_params=_cparams("parallel", "arbitrary"),
        name="gated_merge",
    )(y_attn, y_lru, proj, proj, w_pa, w_pl)


OUT_TN = 512


def _out_proj_kernel(x_ref, m_ref, w_ref, o_ref):
    y = jnp.dot(m_ref[...], w_ref[...].astype(BF16), preferred_element_type=F32)
    for s in range(STRIDE):
        o_ref[:, s, :] = x_ref[:, s, :] + y[s * SLAB:(s + 1) * SLAB, :]


def _out_proj(x4, merged, w_out):
    b = x4.shape[0]
    tok = pl.BlockSpec((None, SLAB, STRIDE, OUT_TN), lambda bi, j: (bi, 0, 0, j))
    return pl.pallas_call(
        _out_proj_kernel,
        grid=(b, D_MODEL // OUT_TN),
        in_specs=[
            tok,
            pl.BlockSpec((SEQ, D_MODEL), lambda bi, j: (bi, 0)),
            pl.BlockSpec((D_MODEL, OUT_TN), lambda bi, j: (0, j)),
        ],
        out_specs=tok,
        out_shape=jax.ShapeDtypeStruct(x4.shape, F32),
        compiler_params=_cparams("parallel", "arbitrary"),
        name="out_proj_residual",
    )(x4, merged, w_out)


MLP_TM = 1024
MLP_TF = 1024
MLP_VMEM_LIMIT = V7X_VMEM_BYTES - 2 * 1024 * 1024


def _mlp_kernel(h_ref, g_ref, wu_ref, wd_ref, gf_ref, o_ref, hn_ref):
    f = pl.program_id(1)

    @pl.when(f == 0)
    def _():
        h = h_ref[...]
        ms = jnp.mean(h * h, axis=-1, keepdims=True)
        hn_ref[...] = (h * lax.rsqrt(ms + EPS) * g_ref[...]).astype(BF16)

    def accumulate(first):
        up = jnp.dot(hn_ref[...], wu_ref[...], preferred_element_type=F32)
        hid = jnp.square(jnp.maximum(up, 0.0)).astype(BF16)
        part = jnp.dot(hid, wd_ref[...], preferred_element_type=F32)
        if first:
            o_ref[...] = part
        else:
            o_ref[...] += part

    pl.when(f == 0)(lambda: accumulate(True))
    pl.when(f > 0)(lambda: accumulate(False))

    @pl.when(f == pl.num_programs(1) - 1)
    def _():
        h2 = h_ref[...] + o_ref[...]
        ms = jnp.mean(h2 * h2, axis=-1, keepdims=True)
        o_ref[...] = h2 * lax.rsqrt(ms + EPS) * gf_ref[...]


def _mlp(h1, g_mlp, w_up, w_down, g_final):
    t = h1.shape[0]
    return pl.pallas_call(
        _mlp_kernel,
        grid=(t // MLP_TM, D_FF // MLP_TF),
        in_specs=[
            pl.BlockSpec((MLP_TM, D_MODEL), lambda i, f: (i, 0)),
            pl.BlockSpec((1, D_MODEL), lambda i, f: (0, 0)),
            pl.BlockSpec((D_MODEL, MLP_TF), lambda i, f: (0, f)),
            pl.BlockSpec((MLP_TF, D_MODEL), lambda i, f: (f, 0)),
            pl.BlockSpec((1, D_MODEL), lambda i, f: (0, 0)),
        ],
        out_specs=pl.BlockSpec((MLP_TM, D_MODEL), lambda i, f: (i, 0)),
        out_shape=jax.ShapeDtypeStruct((t, D_MODEL), F32),
        scratch_shapes=[pltpu.VMEM((MLP_TM, D_MODEL), BF16)],
        compiler_params=pltpu.CompilerParams(dimension_semantics=("parallel", "arbitrary"),
                                             vmem_limit_bytes=MLP_VMEM_LIMIT),
        name="mlp_final_norm",
    )(h1, g_mlp.reshape(1, -1), w_up, w_down, g_final.reshape(1, -1))


def kernel(x, norm_mix_g, w_in, conv_w, conv_b, lru_wa, lru_ba, lru_wx, lru_bx, lru_lambda,
           w_proj_attn, w_proj_lru, w_out, norm_mlp_g, w_up, w_down, norm_final_g):
    assert w_in.shape[0] == 1, "single-layer block"
    batch = x.shape[0]
    x4 = x.reshape(batch, SLAB, STRIDE, D_MODEL)
    proj = _in_proj(_norm_permute(x4, norm_mix_g[0]), w_in[0])
    y_attn, y_lru = _mixers(proj, batch, conv_w[0], conv_b[0], lru_wa[0], lru_ba[0], lru_wx[0], lru_bx[0],
                            lru_lambda[0])
    merged = _merge(y_attn, y_lru, proj, w_proj_attn[0].astype(BF16), w_proj_lru[0].astype(BF16))
    h1 = _out_proj(x4, merged, w_out[0])
    out = _mlp(h1.reshape(batch * SEQ, D_MODEL), norm_mlp_g[0], w_up[0].astype(BF16), w_down[0].astype(BF16),
               norm_final_g)
    return out.reshape(batch, SEQ, D_MODEL)
```

```python
import jax
import jax.numpy as jnp
import numpy as np
from jax import lax
from jax.experimental import pallas as pl
from jax.experimental.pallas import tpu as pltpu

D_MODEL = 2048
SEQ = 2048
HEADS = 16
HEAD_DIM = 128
D_FF = 4 * D_MODEL
IN_COLS = 7 * D_MODEL
CONV_W = 4
LRU_C = 8.0
EPS = 1e-6

STRIDE = 16
SLAB = SEQ // STRIDE
HALF = STRIDE // 2
MASKED = -1e30
LOG2E = 1.4426950408889634
Q_SCALE = HEAD_DIM ** -0.5 * LOG2E

V7X_VMEM_BYTES = 64 * 1024 * 1024
VMEM_LIMIT = V7X_VMEM_BYTES - 8 * 1024 * 1024

BF16 = jnp.bfloat16
F32 = jnp.float32


def _cparams(*sem):
    return pltpu.CompilerParams(dimension_semantics=sem, vmem_limit_bytes=VMEM_LIMIT)


def _sigmoid(x):
    return 1.0 / (1.0 + jnp.exp2(x * (-LOG2E)))


def _gelu_tanh(x):
    c = 0.7978845608028654
    return x * (0.5 + 0.5 * jnp.tanh(x * (c + (c * 0.044715) * (x * x))))


NORM_TM = HALF * SLAB
IN_TM = SEQ
IN_TN = 1024
LRU_GATE_SPLIT = 4


def _norm_kernel(x_hbm, g_ref, o_ref, x_buf, x_sem):
    b = pl.program_id(0)
    i = pl.program_id(1)
    step = b * 2 + i
    last = pl.num_programs(0) * 2 - 1
    slot = lax.rem(step, 2)

    def gather(bi, half, sl, s):
        return pltpu.make_async_copy(x_hbm.at[bi, :, half * HALF + s, :], x_buf.at[sl, s], x_sem.at[sl])

    @pl.when(step == 0)
    def _():
        for s in range(HALF):
            gather(b, i, slot, s).start()

    @pl.when(step < last)
    def _():
        for s in range(HALF):
            gather(b + i, 1 - i, 1 - slot, s).start()

    for s in range(HALF):
        gather(b, i, slot, s).wait()
    for s in range(HALF):
        xs = x_buf[slot, s]
        ms = jnp.mean(xs * xs, axis=-1, keepdims=True)
        o_ref[s * SLAB:(s + 1) * SLAB, :] = (xs * lax.rsqrt(ms + EPS) * g_ref[...]).astype(o_ref.dtype)


def _norm_permute(x4, g):
    b = x4.shape[0]
    return pl.pallas_call(
        _norm_kernel,
        grid=(b, 2),
        in_specs=[
            pl.BlockSpec(memory_space=pl.ANY),
            pl.BlockSpec((1, D_MODEL), lambda bi, i: (0, 0)),
        ],
        out_specs=pl.BlockSpec((NORM_TM, D_MODEL), lambda bi, i: (bi * 2 + i, 0)),
        out_shape=jax.ShapeDtypeStruct((b * SEQ, D_MODEL), BF16),
        scratch_shapes=[pltpu.VMEM((2, HALF, SLAB, D_MODEL), F32), pltpu.SemaphoreType.DMA((2,))],
        compiler_params=_cparams("arbitrary", "arbitrary"),
        name="norm_permute",
    )(x4, g.reshape(1, D_MODEL))


def _in_proj_kernel(xn_ref, w_ref, o_ref):
    j = pl.program_id(1)
    tiles = D_MODEL // IN_TN
    is_q = j < tiles
    is_lru_gate = jnp.logical_and(j >= LRU_GATE_SPLIT * tiles, j < (LRU_GATE_SPLIT + 1) * tiles)

    def project(epilogue):
        y = jnp.dot(xn_ref[...], w_ref[...].astype(BF16), preferred_element_type=F32)
        o_ref[...] = epilogue(y).astype(o_ref.dtype)

    pl.when(is_q)(lambda: project(lambda y: y * Q_SCALE))
    pl.when(is_lru_gate)(lambda: project(_gelu_tanh))
    pl.when(jnp.logical_not(jnp.logical_or(is_q, is_lru_gate)))(lambda: project(lambda y: y))


def _in_proj(xn, w):
    t = xn.shape[0]
    return pl.pallas_call(
        _in_proj_kernel,
        grid=(t // IN_TM, IN_COLS // IN_TN),
        in_specs=[
            pl.BlockSpec((IN_TM, D_MODEL), lambda i, j: (i, 0)),
            pl.BlockSpec((D_MODEL, IN_TN), lambda i, j: (0, j)),
        ],
        out_specs=pl.BlockSpec((IN_TM, IN_TN), lambda i, j: (i, j)),
        out_shape=jax.ShapeDtypeStruct((t, IN_COLS), BF16),
        compiler_params=_cparams("parallel", "arbitrary"),
        name="in_proj",
    )(xn, w)


def _bias_tables():
    slopes = 2.0 ** (-8.0 * np.arange(1, HEADS + 1, dtype=np.float64) / HEADS)

    def table(tq, tk, span):
        dist = (tq[:, None] - tk[None, :]).astype(np.float64)
        ok = (dist >= 0) & (dist <= span)
        bias = -slopes[:, None, None] * dist[None] * LOG2E
        return np.where(ok[None], bias, MASKED).astype(np.float32)

    u = np.arange(SLAB)
    b16 = table(16 * u, 16 * u, 16 * 128)
    m4, ul = np.meshgrid(np.arange(4), np.arange(32), indexing="ij")
    m4k, ulk = np.meshgrid(np.arange(4), np.arange(64), indexing="ij")
    tk4 = (16 * ulk + 4 * m4k).reshape(-1)
    b4_first = table((16 * ul + 4 * m4).reshape(-1), tk4, 4 * 128)
    b4 = table((16 * (32 + ul) + 4 * m4).reshape(-1), tk4, 4 * 128)
    r1, ul1 = np.meshgrid(np.arange(16), np.arange(D1_Q), indexing="ij")
    r1k, ul1k = np.meshgrid(np.arange(16), np.arange(2 * D1_Q), indexing="ij")
    tk1 = (16 * ul1k + r1k).reshape(-1)
    b1_first = table((16 * ul1 + r1).reshape(-1), tk1, 128)
    b1 = table((16 * (D1_Q + ul1) + r1).reshape(-1), tk1, 128)
    return b16, b4_first, b4, b1_first, b1


def _rows(ref, starts, size):
    return jnp.concatenate([ref[s:s + size, :] for s in starts], axis=0)


def _scatter(ref, starts, size, val):
    for i, st in enumerate(starts):
        ref[st:st + size, :] = val[i * size:(i + 1) * size, :]


def _d4_blocks():
    for r4 in range(4):
        slabs = [(r4 + 4 * m) * SLAB for m in range(4)]
        for n in range(4):
            yield [b + 32 * n for b in slabs], [b + 32 * max(n - 1, 0) for b in slabs], n == 0


D1_Q = 8


def _d1_blocks():
    for n in range(SLAB // D1_Q):
        yield ([r * SLAB + D1_Q * n for r in range(STRIDE)],
               [r * SLAB + D1_Q * max(n - 1, 0) for r in range(STRIDE)], n == 0)


def _attn_body(q_ref, k_ref, v_ref, b16_ref, b4f_ref, b4_ref, b1f_ref, b1_ref, o_ref,
               s16, s4, s1, p16, p4, p1, a16, a4, linv):
    def qk(q, k):
        return lax.dot_general(q, k, (((1,), (1,)), ((), ())), preferred_element_type=F32)

    for r in range(STRIDE):
        rows = slice(r * SLAB, (r + 1) * SLAB)
        s16[rows, :] = qk(q_ref[rows, :], k_ref[rows, :]) + b16_ref[...]
        yield
    for q_st, k_st, first in _d4_blocks():
        s = qk(_rows(q_ref, q_st, 32), _rows(k_ref, k_st, 64)) + (b4f_ref if first else b4_ref)[...]
        _scatter(s4, q_st, 32, s)
        yield
    for q_st, k_st, first in _d1_blocks():
        s = qk(_rows(q_ref, q_st, D1_Q), _rows(k_ref, k_st, 2 * D1_Q)) + (b1f_ref if first else b1_ref)[...]
        _scatter(s1, q_st, D1_Q, s)
        yield

    for r in range(STRIDE):
        rows = slice(r * SLAB, (r + 1) * SLAB)
        x16, x4, x1 = s16[rows, :], s4[rows, :], s1[rows, :]
        m = jnp.maximum(jnp.maximum(jnp.max(x16, axis=-1, keepdims=True), jnp.max(x4, axis=-1, keepdims=True)),
                        jnp.max(x1, axis=-1, keepdims=True))
        e16 = jnp.exp2(x16 - m)
        e4 = jnp.exp2(x4 - m)
        e1 = jnp.exp2(x1 - m)
        l = (jnp.sum(e16, axis=-1, keepdims=True) + jnp.sum(e4, axis=-1, keepdims=True)
             + jnp.sum(e1, axis=-1, keepdims=True))
        p16[rows, :] = e16.astype(BF16)
        p4[rows, :] = e4.astype(BF16)
        p1[rows, :] = e1.astype(BF16)
        linv[rows, :] = jnp.broadcast_to(1.0 / l, (SLAB, HEAD_DIM))
        yield

    for r in range(STRIDE):
        rows = slice(r * SLAB, (r + 1) * SLAB)
        a16[rows, :] = jnp.dot(p16[rows, :], v_ref[rows, :], preferred_element_type=F32)
        yield
    for q_st, k_st, _ in _d4_blocks():
        _scatter(a4, q_st, 32, jnp.dot(_rows(p4, q_st, 32), _rows(v_ref, k_st, 64), preferred_element_type=F32))
        yield
    for q_st, k_st, _ in _d1_blocks():
        acc = jnp.dot(_rows(p1, q_st, D1_Q), _rows(v_ref, k_st, 2 * D1_Q), preferred_element_type=F32)
        acc = acc + _rows(a16, q_st, D1_Q) + _rows(a4, q_st, D1_Q)
        _scatter(o_ref, q_st, D1_Q, (acc * _rows(linv, q_st, D1_Q)).astype(o_ref.dtype))
        yield


def _shift_rows(x, k, fill):
    n = x.shape[0]
    if k % 8 == 0:
        return jnp.concatenate([jnp.full((k, x.shape[1]), fill, x.dtype), x[:n - k, :]], axis=0)
    rolled = pltpu.roll(x, k, axis=0)
    row = lax.broadcasted_iota(jnp.int32, x.shape, 0)
    return jnp.where(row < k, fill, rolled)


def _lru_body(xr_ref, gg_ref, cw_ref, cb_ref, wa_ref, ba_ref, wx_ref, bx_ref, lam_ref,
              o_ref, x32_ref, a_ref, b_ref):
    cw = cw_ref[...]
    cb = cb_ref[...]
    x32_ref[...] = xr_ref[...].astype(F32)

    def tap(r):
        x = x32_ref[(r % STRIDE) * SLAB:(r % STRIDE + 1) * SLAB, :]
        return x if r >= 0 else _shift_rows(x, 1, 0.0)

    w_gates = jnp.concatenate([wa_ref[...], wx_ref[...]], axis=1).astype(BF16)
    z = -lam_ref[...]
    softplus = jnp.maximum(z, 0.0) + jnp.log1p(jnp.exp(-jnp.abs(z)))
    c8 = (-LRU_C) * softplus

    ta = tb = None
    for r in range(STRIDE):
        cur = slice(r * SLAB, (r + 1) * SLAB)
        xc = cb + cw[3:4, :] * tap(r)
        for j in range(1, CONV_W):
            xc = xc + cw[3 - j:4 - j, :] * tap(r - j)
        gates = jnp.dot(xc.astype(BF16), w_gates, preferred_element_type=F32)
        rg = _sigmoid(gates[:, :HEAD_DIM] + ba_ref[...])
        ig = _sigmoid(gates[:, HEAD_DIM:] + bx_ref[...])
        log_a = rg * c8
        a = jnp.exp(log_a)
        m2 = -jnp.tanh(log_a) * (a * a + 1.0)
        mult = jnp.where(m2 == 0.0, 0.0, m2 * lax.rsqrt(m2))
        b = mult * (ig * xc)
        if r > 0:
            b = a * tb + b
            a = a * ta
        a_ref[cur, :] = a
        b_ref[cur, :] = b
        ta, tb = a, b
        yield

    k = 1
    while k < SLAB:
        tb = ta * _shift_rows(tb, k, 0.0) + tb
        ta = ta * _shift_rows(ta, k, 1.0)
        k *= 2
    carry = _shift_rows(tb, 1, 0.0)
    yield

    for r in range(STRIDE):
        cur = slice(r * SLAB, (r + 1) * SLAB)
        h = b_ref[cur, :] + a_ref[cur, :] * carry
        o_ref[cur, :] = (h * gg_ref[cur, :].astype(F32)).astype(o_ref.dtype)
        yield


N_ATTN_IN = 8
N_LRU_IN = 9
N_ATTN_SCRATCH = 9
ATTN_ITEMS = 2 * (STRIDE + 16 + SLAB // D1_Q) + STRIDE
LRU_ITEMS = 2 * STRIDE + 1


def _mixers_kernel(*refs):
    attn_in = refs[:N_ATTN_IN]
    lru_in = refs[N_ATTN_IN:N_ATTN_IN + N_LRU_IN]
    o_attn, o_lru = refs[N_ATTN_IN + N_LRU_IN:N_ATTN_IN + N_LRU_IN + 2]
    scratch = refs[N_ATTN_IN + N_LRU_IN + 2:]
    attn = _attn_body(*attn_in, o_attn, *scratch[:N_ATTN_SCRATCH])
    lru = _lru_body(*lru_in, o_lru, *scratch[N_ATTN_SCRATCH:])
    emitted = 0
    for i in range(ATTN_ITEMS):
        next(attn)
        while emitted * ATTN_ITEMS < (i + 1) * LRU_ITEMS:
            next(lru)
            emitted += 1
    assert next(attn, None) is None and next(lru, None) is None


def _mixers(proj, batch, conv_w, conv_b, wa, ba, wx, bx, lam):
    tables = [jnp.asarray(t) for t in _bias_tables()]
    per_head = lambda t: pl.BlockSpec((None,) + t.shape[1:], lambda h, b: (h, 0, 0))
    col = lambda off: pl.BlockSpec((SEQ, HEAD_DIM), lambda h, b: (b, off + h))
    vec = lambda rows: pl.BlockSpec((rows, HEAD_DIM), lambda h, b: (0, h))
    mat = pl.BlockSpec((None, HEAD_DIM, HEAD_DIM), lambda h, b: (h, 0, 0))
    out = pl.BlockSpec((SEQ, HEAD_DIM), lambda h, b: (b, h))
    f32_rows = lambda w: pltpu.VMEM((SEQ, w), F32)
    bf16_rows = lambda w: pltpu.VMEM((SEQ, w), BF16)
    return pl.pallas_call(
        _mixers_kernel,
        grid=(HEADS, batch),
        in_specs=[col(0), col(HEADS), col(2 * HEADS)] + [per_head(t) for t in tables]
                 + [col(3 * HEADS), col(4 * HEADS), vec(CONV_W), vec(1), mat, vec(1), mat, vec(1), vec(1)],
        out_specs=[out, out],
        out_shape=[jax.ShapeDtypeStruct((batch * SEQ, D_MODEL), BF16)] * 2,
        scratch_shapes=[f32_rows(128), f32_rows(256), f32_rows(256), bf16_rows(128), bf16_rows(256), bf16_rows(256),
                        f32_rows(HEAD_DIM), f32_rows(HEAD_DIM), f32_rows(HEAD_DIM)]
                       + [f32_rows(HEAD_DIM)] * 3,
        compiler_params=_cparams("parallel", "parallel"),
        name="token_mixers",
    )(proj, proj, proj, *tables, proj, proj, conv_w, conv_b.reshape(1, -1), wa, ba.reshape(1, -1), wx,
      bx.reshape(1, -1), lam.reshape(1, -1))


MG_TM = 1024
MG_TN = 1024


def _merge_kernel(ya_ref, yl_ref, ga_ref, gl_ref, wa_ref, wl_ref, o_ref):
    pa = jnp.dot(ya_ref[...], wa_ref[...], preferred_element_type=F32)
    pl_ = jnp.dot(yl_ref[...], wl_ref[...], preferred_element_type=F32)
    merged = _sigmoid(ga_ref[...].astype(F32)) * pa + _sigmoid(gl_ref[...].astype(F32)) * pl_
    o_ref[...] = merged.astype(o_ref.dtype)


def _merge(y_attn, y_lru, proj, w_pa, w_pl):
    t = y_attn.shape[0]
    nj = D_MODEL // MG_TN
    act = pl.BlockSpec((MG_TM, D_MODEL), lambda i, j: (i, 0))
    gate = lambda off: pl.BlockSpec((MG_TM, MG_TN), lambda i, j: (i, off * nj + j))
    w = pl.BlockSpec((D_MODEL, MG_TN), lambda i, j: (0, j))
    return pl.pallas_call(
        _merge_kernel,
        grid=(t // MG_TM, nj),
        in_specs=[act, act, gate(5), gate(6), w, w],
        out_specs=pl.BlockSpec((MG_TM, MG_TN), lambda i, j: (i, j)),
        out_shape=jax.ShapeDtypeStruct((t, D_MODEL), BF16),
        compiler_params=_cparams("parallel", "arbitrary"),
        name="gated_merge",
    )(y_attn, y_lru, proj, proj, w_pa, w_pl)


OUT_TN = 512


def _out_proj_kernel(x_hbm, m_ref, w_ref, h_hbm, x_buf, h_buf, x_sem, h_sem):
    b = pl.program_id(0)
    j = pl.program_id(1)
    nb = pl.num_programs(0)
    nj = pl.num_programs(1)
    step = b * nj + j
    slot = lax.rem(step, 2)

    def gather(r):
        cols = pl.ds(pl.multiple_of(j * OUT_TN, OUT_TN), OUT_TN)
        return pltpu.make_async_copy(x_hbm.at[b, :, r, cols], x_buf.at[r], x_sem)

    def scatter(bi, ji, r, sl):
        cols = pl.ds(pl.multiple_of(ji * OUT_TN, OUT_TN), OUT_TN)
        return pltpu.make_async_copy(h_buf.at[sl, r], h_hbm.at[bi, :, r, cols], h_sem.at[sl])

    for r in range(STRIDE):
        gather(r).start()
    y = jnp.dot(m_ref[...], w_ref[...].astype(BF16), preferred_element_type=F32)
    for r in range(STRIDE):
        gather(r).wait()
    for r in range(STRIDE):
        h_buf[slot, r] = x_buf[r] + y[r * SLAB:(r + 1) * SLAB, :]

    @pl.when(step > 0)
    def _():
        prev_b = jnp.where(j == 0, b - 1, b)
        prev_j = jnp.where(j == 0, nj - 1, j - 1)
        for r in range(STRIDE):
            scatter(prev_b, prev_j, r, 1 - slot).wait()

    for r in range(STRIDE):
        scatter(b, j, r, slot).start()

    @pl.when(step == nb * nj - 1)
    def _():
        for r in range(STRIDE):
            scatter(b, j, r, slot).wait()


def _out_proj(x4, merged, w_out):
    b = x4.shape[0]
    return pl.pallas_call(
        _out_proj_kernel,
        grid=(b, D_MODEL // OUT_TN),
        in_specs=[
            pl.BlockSpec(memory_space=pl.ANY),
            pl.BlockSpec((SEQ, D_MODEL), lambda bi, j: (bi, 0)),
            pl.BlockSpec((D_MODEL, OUT_TN), lambda bi, j: (0, j)),
        ],
        out_specs=pl.BlockSpec(memory_space=pl.ANY),
        out_shape=jax.ShapeDtypeStruct(x4.shape, F32),
        scratch_shapes=[pltpu.VMEM((STRIDE, SLAB, OUT_TN), F32), pltpu.VMEM((2, STRIDE, SLAB, OUT_TN), F32),
                        pltpu.SemaphoreType.DMA(()), pltpu.SemaphoreType.DMA((2,))],
        compiler_params=_cparams("arbitrary", "arbitrary"),
        name="out_proj_residual",
    )(x4, merged, w_out)


MLP_TM = 1024
MLP_TF = 1024
MLP_VMEM_LIMIT = V7X_VMEM_BYTES - 2 * 1024 * 1024


def _mlp_kernel(h_ref, g_ref, wu_ref, wd_ref, gf_ref, o_ref, hn_ref):
    f = pl.program_id(1)

    @pl.when(f == 0)
    def _():
        h = h_ref[...]
        ms = jnp.mean(h * h, axis=-1, keepdims=True)
        hn_ref[...] = (h * lax.rsqrt(ms + EPS) * g_ref[...]).astype(BF16)

    def accumulate(first):
        up = jnp.dot(hn_ref[...], wu_ref[...], preferred_element_type=F32)
        hid = jnp.square(jnp.maximum(up, 0.0)).astype(BF16)
        part = jnp.dot(hid, wd_ref[...], preferred_element_type=F32)
        if first:
            o_ref[...] = part
        else:
            o_ref[...] += part

    pl.when(f == 0)(lambda: accumulate(True))
    pl.when(f > 0)(lambda: accumulate(False))

    @pl.when(f == pl.num_programs(1) - 1)
    def _():
        h2 = h_ref[...] + o_ref[...]
        ms = jnp.mean(h2 * h2, axis=-1, keepdims=True)
        o_ref[...] = h2 * lax.rsqrt(ms + EPS) * gf_ref[...]


def _mlp(h1, g_mlp, w_up, w_down, g_final):
    t = h1.shape[0]
    return pl.pallas_call(
        _mlp_kernel,
        grid=(t // MLP_TM, D_FF // MLP_TF),
        in_specs=[
            pl.BlockSpec((MLP_TM, D_MODEL), lambda i, f: (i, 0)),
            pl.BlockSpec((1, D_MODEL), lambda i, f: (0, 0)),
            pl.BlockSpec((D_MODEL, MLP_TF), lambda i, f: (0, f)),
            pl.BlockSpec((MLP_TF, D_MODEL), lambda i, f: (f, 0)),
            pl.BlockSpec((1, D_MODEL), lambda i, f: (0, 0)),
        ],
        out_specs=pl.BlockSpec((MLP_TM, D_MODEL), lambda i, f: (i, 0)),
        out_shape=jax.ShapeDtypeStruct((t, D_MODEL), F32),
        scratch_shapes=[pltpu.VMEM((MLP_TM, D_MODEL), BF16)],
        compiler_params=pltpu.CompilerParams(dimension_semantics=("parallel", "arbitrary"),
                                             vmem_limit_bytes=MLP_VMEM_LIMIT),
        name="mlp_final_norm",
    )(h1, g_mlp.reshape(1, -1), w_up, w_down, g_final.reshape(1, -1))


def kernel(x, norm_mix_g, w_in, conv_w, conv_b, lru_wa, lru_ba, lru_wx, lru_bx, lru_lambda,
           w_proj_attn, w_proj_lru, w_out, norm_mlp_g, w_up, w_down, norm_final_g):
    assert w_in.shape[0] == 1, "single-layer block"
    batch = x.shape[0]
    x4 = x.reshape(batch, SLAB, STRIDE, D_MODEL)
    proj = _in_proj(_norm_permute(x4, norm_mix_g[0]), w_in[0])
    y_attn, y_lru = _mixers(proj, batch, conv_w[0], conv_b[0], lru_wa[0], lru_ba[0], lru_wx[0], lru_bx[0],
                            lru_lambda[0])
    merged = _merge(y_attn, y_lru, proj, w_proj_attn[0].astype(BF16), w_proj_lru[0].astype(BF16))
    h1 = _out_proj(x4, merged, w_out[0])
    out = _mlp(h1.reshape(batch * SEQ, D_MODEL), norm_mlp_g[0], w_up[0].astype(BF16), w_down[0].astype(BF16),
               norm_final_g)
    return out.reshape(batch, SEQ, D_MODEL)
```

```python
import jax
import jax.numpy as jnp
import numpy as np
from jax import lax
from jax.experimental import pallas as pl
from jax.experimental.pallas import tpu as pltpu

D_MODEL = 2048
SEQ = 2048
HEADS = 16
HEAD_DIM = 128
D_FF = 4 * D_MODEL
IN_COLS = 7 * D_MODEL
CONV_W = 4
LRU_C = 8.0
EPS = 1e-6

STRIDE = 16
SLAB = SEQ // STRIDE
HALF = STRIDE // 2
MASKED = -1e30
LOG2E = 1.4426950408889634
Q_SCALE = HEAD_DIM ** -0.5 * LOG2E

V7X_VMEM_BYTES = 64 * 1024 * 1024
VMEM_LIMIT = V7X_VMEM_BYTES - 8 * 1024 * 1024

BF16 = jnp.bfloat16
F32 = jnp.float32


def _cparams(*sem):
    return pltpu.CompilerParams(dimension_semantics=sem, vmem_limit_bytes=VMEM_LIMIT)


def _sigmoid(x):
    return 1.0 / (1.0 + jnp.exp2(x * (-LOG2E)))


def _gelu_tanh(x):
    c = 0.7978845608028654
    return x * (0.5 + 0.5 * jnp.tanh(x * (c + (c * 0.044715) * (x * x))))


NORM_TM = HALF * SLAB
IN_TM = SEQ
IN_TN = 1024
LRU_GATE_SPLIT = 4


def _norm_kernel(x_hbm, g_ref, o_ref, x_buf, x_sem):
    b = pl.program_id(0)
    i = pl.program_id(1)
    step = b * 2 + i
    last = pl.num_programs(0) * 2 - 1
    slot = lax.rem(step, 2)

    def gather(bi, half, sl, s):
        return pltpu.make_async_copy(x_hbm.at[bi, :, half * HALF + s, :], x_buf.at[sl, s], x_sem.at[sl])

    @pl.when(step == 0)
    def _():
        for s in range(HALF):
            gather(b, i, slot, s).start()

    @pl.when(step < last)
    def _():
        for s in range(HALF):
            gather(b + i, 1 - i, 1 - slot, s).start()

    for s in range(HALF):
        gather(b, i, slot, s).wait()
    for s in range(HALF):
        xs = x_buf[slot, s]
        ms = jnp.mean(xs * xs, axis=-1, keepdims=True)
        o_ref[s * SLAB:(s + 1) * SLAB, :] = (xs * lax.rsqrt(ms + EPS) * g_ref[...]).astype(o_ref.dtype)


def _norm_permute(x4, g):
    b = x4.shape[0]
    return pl.pallas_call(
        _norm_kernel,
        grid=(b, 2),
        in_specs=[
            pl.BlockSpec(memory_space=pl.ANY),
            pl.BlockSpec((1, D_MODEL), lambda bi, i: (0, 0)),
        ],
        out_specs=pl.BlockSpec((NORM_TM, D_MODEL), lambda bi, i: (bi * 2 + i, 0)),
        out_shape=jax.ShapeDtypeStruct((b * SEQ, D_MODEL), BF16),
        scratch_shapes=[pltpu.VMEM((2, HALF, SLAB, D_MODEL), F32), pltpu.SemaphoreType.DMA((2,))],
        compiler_params=_cparams("arbitrary", "arbitrary"),
        name="norm_permute",
    )(x4, g.reshape(1, D_MODEL))


def _in_proj_kernel(xn_ref, w_ref, o_ref):
    j = pl.program_id(1)
    tiles = D_MODEL // IN_TN
    is_q = j < tiles
    is_lru_gate = jnp.logical_and(j >= LRU_GATE_SPLIT * tiles, j < (LRU_GATE_SPLIT + 1) * tiles)

    def project(epilogue):
        y = jnp.dot(xn_ref[...], w_ref[...].astype(BF16), preferred_element_type=F32)
        o_ref[...] = epilogue(y).astype(o_ref.dtype)

    pl.when(is_q)(lambda: project(lambda y: y * Q_SCALE))
    pl.when(is_lru_gate)(lambda: project(_gelu_tanh))
    pl.when(jnp.logical_not(jnp.logical_or(is_q, is_lru_gate)))(lambda: project(lambda y: y))


def _in_proj(xn, w):
    t = xn.shape[0]
    return pl.pallas_call(
        _in_proj_kernel,
        grid=(t // IN_TM, IN_COLS // IN_TN),
        in_specs=[
            pl.BlockSpec((IN_TM, D_MODEL), lambda i, j: (i, 0)),
            pl.BlockSpec((D_MODEL, IN_TN), lambda i, j: (0, j)),
        ],
        out_specs=pl.BlockSpec((IN_TM, IN_TN), lambda i, j: (i, j)),
        out_shape=jax.ShapeDtypeStruct((t, IN_COLS), BF16),
        compiler_params=_cparams("parallel", "arbitrary"),
        name="in_proj",
    )(xn, w)


def _bias_tables():
    slopes = 2.0 ** (-8.0 * np.arange(1, HEADS + 1, dtype=np.float64) / HEADS)

    def table(tq, tk, span):
        dist = (tq[:, None] - tk[None, :]).astype(np.float64)
        ok = (dist >= 0) & (dist <= span)
        bias = -slopes[:, None, None] * dist[None] * LOG2E
        return np.where(ok[None], bias, MASKED).astype(np.float32)

    u = np.arange(SLAB)
    b16 = table(16 * u, 16 * u, 16 * 128)
    m4, ul = np.meshgrid(np.arange(4), np.arange(32), indexing="ij")
    m4k, ulk = np.meshgrid(np.arange(4), np.arange(64), indexing="ij")
    tk4 = (16 * ulk + 4 * m4k).reshape(-1)
    b4_first = table((16 * ul + 4 * m4).reshape(-1), tk4, 4 * 128)
    b4 = table((16 * (32 + ul) + 4 * m4).reshape(-1), tk4, 4 * 128)
    r1, ul1 = np.meshgrid(np.arange(16), np.arange(D1_Q), indexing="ij")
    r1k, ul1k = np.meshgrid(np.arange(16), np.arange(2 * D1_Q), indexing="ij")
    tk1 = (16 * ul1k + r1k).reshape(-1)
    b1_first = table((16 * ul1 + r1).reshape(-1), tk1, 128)
    b1 = table((16 * (D1_Q + ul1) + r1).reshape(-1), tk1, 128)
    return b16, b4_first, b4, b1_first, b1


def _rows(ref, starts, size):
    return jnp.concatenate([ref[s:s + size, :] for s in starts], axis=0)


def _scatter(ref, starts, size, val):
    for i, st in enumerate(starts):
        ref[st:st + size, :] = val[i * size:(i + 1) * size, :]


def _d4_blocks():
    for r4 in range(4):
        slabs = [(r4 + 4 * m) * SLAB for m in range(4)]
        for n in range(4):
            yield [b + 32 * n for b in slabs], [b + 32 * max(n - 1, 0) for b in slabs], n == 0


D1_Q = 8


def _d1_blocks():
    for n in range(SLAB // D1_Q):
        yield ([r * SLAB + D1_Q * n for r in range(STRIDE)],
               [r * SLAB + D1_Q * max(n - 1, 0) for r in range(STRIDE)], n == 0)


def _attn_body(q_ref, k_ref, v_ref, b16_ref, b4f_ref, b4_ref, b1f_ref, b1_ref, o_ref,
               s16, s4, s1, p16, p4, p1, a16, a4, linv):
    def qk(q, k):
        return lax.dot_general(q, k, (((1,), (1,)), ((), ())), preferred_element_type=F32)

    for r in range(STRIDE):
        rows = slice(r * SLAB, (r + 1) * SLAB)
        s16[rows, :] = qk(q_ref[rows, :], k_ref[rows, :]) + b16_ref[...]
        yield
    for q_st, k_st, first in _d4_blocks():
        s = qk(_rows(q_ref, q_st, 32), _rows(k_ref, k_st, 64)) + (b4f_ref if first else b4_ref)[...]
        _scatter(s4, q_st, 32, s)
        yield
    for q_st, k_st, first in _d1_blocks():
        s = qk(_rows(q_ref, q_st, D1_Q), _rows(k_ref, k_st, 2 * D1_Q)) + (b1f_ref if first else b1_ref)[...]
        _scatter(s1, q_st, D1_Q, s)
        yield

    for r in range(STRIDE):
        rows = slice(r * SLAB, (r + 1) * SLAB)
        x16, x4, x1 = s16[rows, :], s4[rows, :], s1[rows, :]
        m = jnp.maximum(jnp.maximum(jnp.max(x16, axis=-1, keepdims=True), jnp.max(x4, axis=-1, keepdims=True)),
                        jnp.max(x1, axis=-1, keepdims=True))
        e16 = jnp.exp2(x16 - m)
        e4 = jnp.exp2(x4 - m)
        e1 = jnp.exp2(x1 - m)
        l = (jnp.sum(e16, axis=-1, keepdims=True) + jnp.sum(e4, axis=-1, keepdims=True)
             + jnp.sum(e1, axis=-1, keepdims=True))
        p16[rows, :] = e16.astype(BF16)
        p4[rows, :] = e4.astype(BF16)
        p1[rows, :] = e1.astype(BF16)
        linv[rows, :] = jnp.broadcast_to(1.0 / l, (SLAB, HEAD_DIM))
        yield

    for r in range(STRIDE):
        rows = slice(r * SLAB, (r + 1) * SLAB)
        a16[rows, :] = jnp.dot(p16[rows, :], v_ref[rows, :], preferred_element_type=F32)
        yield
    for q_st, k_st, _ in _d4_blocks():
        _scatter(a4, q_st, 32, jnp.dot(_rows(p4, q_st, 32), _rows(v_ref, k_st, 64), preferred_element_type=F32))
        yield
    for q_st, k_st, _ in _d1_blocks():
        acc = jnp.dot(_rows(p1, q_st, D1_Q), _rows(v_ref, k_st, 2 * D1_Q), preferred_element_type=F32)
        acc = acc + _rows(a16, q_st, D1_Q) + _rows(a4, q_st, D1_Q)
        _scatter(o_ref, q_st, D1_Q, (acc * _rows(linv, q_st, D1_Q)).astype(o_ref.dtype))
        yield


def _shift_rows(x, k, fill):
    n = x.shape[0]
    if k % 8 == 0:
        return jnp.concatenate([jnp.full((k, x.shape[1]), fill, x.dtype), x[:n - k, :]], axis=0)
    rolled = pltpu.roll(x, k, axis=0)
    row = lax.broadcasted_iota(jnp.int32, x.shape, 0)
    return jnp.where(row < k, fill, rolled)


def _lru_body(xr_ref, gg_ref, cw_ref, cb_ref, wa_ref, ba_ref, wx_ref, bx_ref, lam_ref,
              o_ref, x32_ref, a_ref, b_ref):
    cw = cw_ref[...]
    cb = cb_ref[...]
    x32_ref[...] = xr_ref[...].astype(F32)

    def tap(r):
        x = x32_ref[(r % STRIDE) * SLAB:(r % STRIDE + 1) * SLAB, :]
        return x if r >= 0 else _shift_rows(x, 1, 0.0)

    w_gates = jnp.concatenate([wa_ref[...], wx_ref[...]], axis=1).astype(BF16)
    z = -lam_ref[...]
    softplus = jnp.maximum(z, 0.0) + jnp.log1p(jnp.exp(-jnp.abs(z)))
    c8 = (-LRU_C) * softplus

    ta = tb = None
    for r in range(STRIDE):
        cur = slice(r * SLAB, (r + 1) * SLAB)
        xc = cb + cw[3:4, :] * tap(r)
        for j in range(1, CONV_W):
            xc = xc + cw[3 - j:4 - j, :] * tap(r - j)
        gates = jnp.dot(xc.astype(BF16), w_gates, preferred_element_type=F32)
        rg = _sigmoid(gates[:, :HEAD_DIM] + ba_ref[...])
        ig = _sigmoid(gates[:, HEAD_DIM:] + bx_ref[...])
        log_a = rg * c8
        a = jnp.exp(log_a)
        m2 = -jnp.tanh(log_a) * (a * a + 1.0)
        mult = jnp.where(m2 == 0.0, 0.0, m2 * lax.rsqrt(m2))
        b = mult * (ig * xc)
        if r > 0:
            b = a * tb + b
            a = a * ta
        a_ref[cur, :] = a
        b_ref[cur, :] = b
        ta, tb = a, b
        yield

    k = 1
    while k < SLAB:
        tb = ta * _shift_rows(tb, k, 0.0) + tb
        ta = ta * _shift_rows(ta, k, 1.0)
        k *= 2
    carry = _shift_rows(tb, 1, 0.0)
    yield

    for r in range(STRIDE):
        cur = slice(r * SLAB, (r + 1) * SLAB)
        h = b_ref[cur, :] + a_ref[cur, :] * carry
        o_ref[cur, :] = (h * gg_ref[cur, :].astype(F32)).astype(o_ref.dtype)
        yield


N_ATTN_IN = 8
N_LRU_IN = 9
N_ATTN_SCRATCH = 9
ATTN_ITEMS = 2 * (STRIDE + 16 + SLAB // D1_Q) + STRIDE
LRU_ITEMS = 2 * STRIDE + 1


def _mixers_kernel(*refs):
    attn_in = refs[:N_ATTN_IN]
    lru_in = refs[N_ATTN_IN:N_ATTN_IN + N_LRU_IN]
    o_attn, o_lru = refs[N_ATTN_IN + N_LRU_IN:N_ATTN_IN + N_LRU_IN + 2]
    scratch = refs[N_ATTN_IN + N_LRU_IN + 2:]
    attn = _attn_body(*attn_in, o_attn, *scratch[:N_ATTN_SCRATCH])
    lru = _lru_body(*lru_in, o_lru, *scratch[N_ATTN_SCRATCH:])
    emitted = 0
    for i in range(ATTN_ITEMS):
        next(attn)
        while emitted * ATTN_ITEMS < (i + 1) * LRU_ITEMS:
            next(lru)
            emitted += 1
    assert next(attn, None) is None and next(lru, None) is None


def _mixers(proj, batch, conv_w, conv_b, wa, ba, wx, bx, lam):
    tables = [jnp.asarray(t) for t in _bias_tables()]
    per_head = lambda t: pl.BlockSpec((None,) + t.shape[1:], lambda h, b: (h, 0, 0))
    col = lambda off: pl.BlockSpec((SEQ, HEAD_DIM), lambda h, b: (b, off + h))
    vec = lambda rows: pl.BlockSpec((rows, HEAD_DIM), lambda h, b: (0, h))
    mat = pl.BlockSpec((None, HEAD_DIM, HEAD_DIM), lambda h, b: (h, 0, 0))
    out = pl.BlockSpec((SEQ, HEAD_DIM), lambda h, b: (b, h))
    f32_rows = lambda w: pltpu.VMEM((SEQ, w), F32)
    bf16_rows = lambda w: pltpu.VMEM((SEQ, w), BF16)
    return pl.pallas_call(
        _mixers_kernel,
        grid=(HEADS, batch),
        in_specs=[col(0), col(HEADS), col(2 * HEADS)] + [per_head(t) for t in tables]
                 + [col(3 * HEADS), col(4 * HEADS), vec(CONV_W), vec(1), mat, vec(1), mat, vec(1), vec(1)],
        out_specs=[out, out],
        out_shape=[jax.ShapeDtypeStruct((batch * SEQ, D_MODEL), BF16)] * 2,
        scratch_shapes=[f32_rows(128), f32_rows(256), f32_rows(256), bf16_rows(128), bf16_rows(256), bf16_rows(256),
                        f32_rows(HEAD_DIM), f32_rows(HEAD_DIM), f32_rows(HEAD_DIM)]
                       + [f32_rows(HEAD_DIM)] * 3,
        compiler_params=_cparams("parallel", "parallel"),
        name="token_mixers",
    )(proj, proj, proj, *tables, proj, proj, conv_w, conv_b.reshape(1, -1), wa, ba.reshape(1, -1), wx,
      bx.reshape(1, -1), lam.reshape(1, -1))


MG_TM = 1024
MG_TN = 1024


def _merge_kernel(ya_ref, yl_ref, ga_ref, gl_ref, wa_ref, wl_ref, o_ref):
    pa = jnp.dot(ya_ref[...], wa_ref[...], preferred_element_type=F32)
    pl_ = jnp.dot(yl_ref[...], wl_ref[...], preferred_element_type=F32)
    merged = _sigmoid(ga_ref[...].astype(F32)) * pa + _sigmoid(gl_ref[...].astype(F32)) * pl_
    o_ref[...] = merged.astype(o_ref.dtype)


def _merge(y_attn, y_lru, proj, w_pa, w_pl):
    t = y_attn.shape[0]
    nj = D_MODEL // MG_TN
    act = pl.BlockSpec((MG_TM, D_MODEL), lambda i, j: (i, 0))
    gate = lambda off: pl.BlockSpec((MG_TM, MG_TN), lambda i, j: (i, off * nj + j))
    w = pl.BlockSpec((D_MODEL, MG_TN), lambda i, j: (0, j))
    return pl.pallas_call(
        _merge_kernel,
        grid=(t // MG_TM, nj),
        in_specs=[act, act, gate(5), gate(6), w, w],
        out_specs=pl.BlockSpec((MG_TM, MG_TN), lambda i, j: (i, j)),
        out_shape=jax.ShapeDtypeStruct((t, D_MODEL), BF16),
        compiler_params=_cparams("parallel", "arbitrary"),
        name="gated_merge",
    )(y_attn, y_lru, proj, proj, w_pa, w_pl)


OUT_TN = 512


def _out_proj_kernel(x_hbm, m_ref, w_ref, h_hbm, x_buf, h_buf, x_sem, h_sem):
    b = pl.program_id(0)
    j = pl.program_id(1)
    nb = pl.num_programs(0)
    nj = pl.num_programs(1)
    step = b * nj + j
    slot = lax.rem(step, 2)

    def gather(r):
        cols = pl.ds(pl.multiple_of(j * OUT_TN, OUT_TN), OUT_TN)
        return pltpu.make_async_copy(x_hbm.at[b, :, r, cols], x_buf.at[r], x_sem)

    def scatter(bi, ji, r, sl):
        cols = pl.ds(pl.multiple_of(ji * OUT_TN, OUT_TN), OUT_TN)
        return pltpu.make_async_copy(h_buf.at[sl, r], h_hbm.at[bi, :, r, cols], h_sem.at[sl])

    for r in range(STRIDE):
        gather(r).start()
    w = w_ref[:, pl.ds(pl.multiple_of(j * OUT_TN, OUT_TN), OUT_TN)]
    y = jnp.dot(m_ref[...], w, preferred_element_type=F32)
    for r in range(STRIDE):
        gather(r).wait()
    for r in range(STRIDE):
        h_buf[slot, r] = x_buf[r] + y[r * SLAB:(r + 1) * SLAB, :]

    @pl.when(step > 0)
    def _():
        prev_b = jnp.where(j == 0, b - 1, b)
        prev_j = jnp.where(j == 0, nj - 1, j - 1)
        for r in range(STRIDE):
            scatter(prev_b, prev_j, r, 1 - slot).wait()

    for r in range(STRIDE):
        scatter(b, j, r, slot).start()

    @pl.when(step == nb * nj - 1)
    def _():
        for r in range(STRIDE):
            scatter(b, j, r, slot).wait()


def _out_proj(x4, merged, w_out):
    b = x4.shape[0]
    return pl.pallas_call(
        _out_proj_kernel,
        grid=(b, D_MODEL // OUT_TN),
        in_specs=[
            pl.BlockSpec(memory_space=pl.ANY),
            pl.BlockSpec((SEQ, D_MODEL), lambda bi, j: (bi, 0)),
            pl.BlockSpec((D_MODEL, D_MODEL), lambda bi, j: (0, 0)),
        ],
        out_specs=pl.BlockSpec(memory_space=pl.ANY),
        out_shape=jax.ShapeDtypeStruct(x4.shape, F32),
        scratch_shapes=[pltpu.VMEM((STRIDE, SLAB, OUT_TN), F32), pltpu.VMEM((2, STRIDE, SLAB, OUT_TN), F32),
                        pltpu.SemaphoreType.DMA(()), pltpu.SemaphoreType.DMA((2,))],
        compiler_params=_cparams("arbitrary", "arbitrary"),
        name="out_proj_residual",
    )(x4, merged, w_out)


MLP_TM = 1024
MLP_TF = 1024
MLP_VMEM_LIMIT = V7X_VMEM_BYTES - 2 * 1024 * 1024


def _mlp_kernel(h_ref, g_ref, wu_ref, wd_ref, gf_ref, o_ref, hn_ref):
    f = pl.program_id(1)

    @pl.when(f == 0)
    def _():
        h = h_ref[...]
        ms = jnp.mean(h * h, axis=-1, keepdims=True)
        hn_ref[...] = (h * lax.rsqrt(ms + EPS) * g_ref[...]).astype(BF16)

    def accumulate(first):
        up = jnp.dot(hn_ref[...], wu_ref[...], preferred_element_type=F32)
        hid = jnp.square(jnp.maximum(up, 0.0)).astype(BF16)
        part = jnp.dot(hid, wd_ref[...], preferred_element_type=F32)
        if first:
            o_ref[...] = part
        else:
            o_ref[...] += part

    pl.when(f == 0)(lambda: accumulate(True))
    pl.when(f > 0)(lambda: accumulate(False))

    @pl.when(f == pl.num_programs(1) - 1)
    def _():
        h2 = h_ref[...] + o_ref[...]
        ms = jnp.mean(h2 * h2, axis=-1, keepdims=True)
        o_ref[...] = h2 * lax.rsqrt(ms + EPS) * gf_ref[...]


def _mlp(h1, g_mlp, w_up, w_down, g_final):
    t = h1.shape[0]
    return pl.pallas_call(
        _mlp_kernel,
        grid=(t // MLP_TM, D_FF // MLP_TF),
        in_specs=[
            pl.BlockSpec((MLP_TM, D_MODEL), lambda i, f: (i, 0)),
            pl.BlockSpec((1, D_MODEL), lambda i, f: (0, 0)),
            pl.BlockSpec((D_MODEL, MLP_TF), lambda i, f: (0, f)),
            pl.BlockSpec((MLP_TF, D_MODEL), lambda i, f: (f, 0)),
            pl.BlockSpec((1, D_MODEL), lambda i, f: (0, 0)),
        ],
        out_specs=pl.BlockSpec((MLP_TM, D_MODEL), lambda i, f: (i, 0)),
        out_shape=jax.ShapeDtypeStruct((t, D_MODEL), F32),
        scratch_shapes=[pltpu.VMEM((MLP_TM, D_MODEL), BF16)],
        compiler_params=pltpu.CompilerParams(dimension_semantics=("parallel", "arbitrary"),
                                             vmem_limit_bytes=MLP_VMEM_LIMIT),
        name="mlp_final_norm",
    )(h1, g_mlp.reshape(1, -1), w_up, w_down, g_final.reshape(1, -1))


def kernel(x, norm_mix_g, w_in, conv_w, conv_b, lru_wa, lru_ba, lru_wx, lru_bx, lru_lambda,
           w_proj_attn, w_proj_lru, w_out, norm_mlp_g, w_up, w_down, norm_final_g):
    assert w_in.shape[0] == 1, "single-layer block"
    batch = x.shape[0]
    x4 = x.reshape(batch, SLAB, STRIDE, D_MODEL)
    proj = _in_proj(_norm_permute(x4, norm_mix_g[0]), w_in[0])
    y_attn, y_lru = _mixers(proj, batch, conv_w[0], conv_b[0], lru_wa[0], lru_ba[0], lru_wx[0], lru_bx[0],
                            lru_lambda[0])
    merged = _merge(y_attn, y_lru, proj, w_proj_attn[0].astype(BF16), w_proj_lru[0].astype(BF16))
    h1 = _out_proj(x4, merged, w_out[0].astype(BF16))
    out = _mlp(h1.reshape(batch * SEQ, D_MODEL), norm_mlp_g[0], w_up[0].astype(BF16), w_down[0].astype(BF16),
               norm_final_g)
    return out.reshape(batch, SEQ, D_MODEL)
```

```python
import jax
import jax.numpy as jnp
import numpy as np
from jax import lax
from jax.experimental import pallas as pl
from jax.experimental.pallas import tpu as pltpu

D_MODEL = 2048
SEQ = 2048
HEADS = 16
HEAD_DIM = 128
D_FF = 4 * D_MODEL
IN_COLS = 7 * D_MODEL
CONV_W = 4
LRU_C = 8.0
EPS = 1e-6

STRIDE = 16
SLAB = SEQ // STRIDE
HALF = STRIDE // 2
MASKED = -1e30
LOG2E = 1.4426950408889634
Q_SCALE = HEAD_DIM ** -0.5 * LOG2E

V7X_VMEM_BYTES = 64 * 1024 * 1024
VMEM_LIMIT = V7X_VMEM_BYTES - 8 * 1024 * 1024

BF16 = jnp.bfloat16
F32 = jnp.float32


def _cparams(*sem):
    return pltpu.CompilerParams(dimension_semantics=sem, vmem_limit_bytes=VMEM_LIMIT)


def _sigmoid(x):
    return 1.0 / (1.0 + jnp.exp2(x * (-LOG2E)))


def _gelu_tanh(x):
    c = 0.7978845608028654
    return x * (0.5 + 0.5 * jnp.tanh(x * (c + (c * 0.044715) * (x * x))))


NORM_TM = HALF * SLAB
IN_TM = SEQ
IN_TN = 1024
LRU_GATE_SPLIT = 4
CAST_STEPS = 8
IN_VMEM_LIMIT = V7X_VMEM_BYTES - 4 * 1024 * 1024


def _norm_kernel(x_hbm, g_ref, o_ref, x_buf, x_sem):
    b = pl.program_id(0)
    i = pl.program_id(1)
    step = b * 2 + i
    last = pl.num_programs(0) * 2 - 1
    slot = lax.rem(step, 2)

    def gather(bi, half, sl, s):
        return pltpu.make_async_copy(x_hbm.at[bi, :, half * HALF + s, :], x_buf.at[sl, s], x_sem.at[sl])

    @pl.when(step == 0)
    def _():
        for s in range(HALF):
            gather(b, i, slot, s).start()

    @pl.when(step < last)
    def _():
        for s in range(HALF):
            gather(b + i, 1 - i, 1 - slot, s).start()

    for s in range(HALF):
        gather(b, i, slot, s).wait()
    for s in range(HALF):
        xs = x_buf[slot, s]
        ms = jnp.mean(xs * xs, axis=-1, keepdims=True)
        o_ref[s * SLAB:(s + 1) * SLAB, :] = (xs * lax.rsqrt(ms + EPS) * g_ref[...]).astype(o_ref.dtype)


def _norm_permute(x4, g):
    b = x4.shape[0]
    return pl.pallas_call(
        _norm_kernel,
        grid=(b, 2),
        in_specs=[
            pl.BlockSpec(memory_space=pl.ANY),
            pl.BlockSpec((1, D_MODEL), lambda bi, i: (0, 0)),
        ],
        out_specs=pl.BlockSpec((NORM_TM, D_MODEL), lambda bi, i: (bi * 2 + i, 0)),
        out_shape=jax.ShapeDtypeStruct((b * SEQ, D_MODEL), BF16),
        scratch_shapes=[pltpu.VMEM((2, HALF, SLAB, D_MODEL), F32), pltpu.SemaphoreType.DMA((2,))],
        compiler_params=_cparams("arbitrary", "arbitrary"),
        name="norm_permute",
    )(x4, g.reshape(1, D_MODEL))


def _in_proj_kernel(xn_ref, w_ref, *refs):
    n_side = (len(refs) - 1) // 2
    side_in, o_ref, side_out = refs[:n_side], refs[n_side], refs[n_side + 1:]
    j = pl.program_id(1)
    tiles = D_MODEL // IN_TN
    is_q = j < tiles
    is_lru_gate = jnp.logical_and(j >= LRU_GATE_SPLIT * tiles, j < (LRU_GATE_SPLIT + 1) * tiles)

    def project(epilogue):
        y = jnp.dot(xn_ref[...], w_ref[...].astype(BF16), preferred_element_type=F32)
        o_ref[...] = epilogue(y).astype(o_ref.dtype)
        for src, dst in zip(side_in, side_out):
            dst[...] = src[...].astype(dst.dtype)

    pl.when(is_q)(lambda: project(lambda y: y * Q_SCALE))
    pl.when(is_lru_gate)(lambda: project(_gelu_tanh))
    pl.when(jnp.logical_not(jnp.logical_or(is_q, is_lru_gate)))(lambda: project(lambda y: y))


def _in_proj(xn, w, later_weights):
    t = xn.shape[0]
    n_i = t // IN_TM
    slices = n_i * CAST_STEPS

    def side_spec(wt):
        rows, rem = divmod(wt.shape[0], slices)
        assert rem == 0 and rows % 16 == 0, wt.shape
        return pl.BlockSpec((rows, wt.shape[1]), lambda i, j: (i * CAST_STEPS + jnp.minimum(j, CAST_STEPS - 1), 0))

    side_specs = [side_spec(wt) for wt in later_weights]
    outs = pl.pallas_call(
        _in_proj_kernel,
        grid=(n_i, IN_COLS // IN_TN),
        in_specs=[
            pl.BlockSpec((IN_TM, D_MODEL), lambda i, j: (i, 0)),
            pl.BlockSpec((D_MODEL, IN_TN), lambda i, j: (0, j)),
        ] + side_specs,
        out_specs=[pl.BlockSpec((IN_TM, IN_TN), lambda i, j: (i, j))] + side_specs,
        out_shape=[jax.ShapeDtypeStruct((t, IN_COLS), BF16)]
                  + [jax.ShapeDtypeStruct(wt.shape, BF16) for wt in later_weights],
        compiler_params=pltpu.CompilerParams(dimension_semantics=("parallel", "arbitrary"),
                                             vmem_limit_bytes=IN_VMEM_LIMIT),
        name="in_proj",
    )(xn, w, *later_weights)
    return outs[0], outs[1:]


def _bias_tables():
    slopes = 2.0 ** (-8.0 * np.arange(1, HEADS + 1, dtype=np.float64) / HEADS)

    def table(tq, tk, span):
        dist = (tq[:, None] - tk[None, :]).astype(np.float64)
        ok = (dist >= 0) & (dist <= span)
        bias = -slopes[:, None, None] * dist[None] * LOG2E
        return np.where(ok[None], bias, MASKED).astype(np.float32)

    u = np.arange(SLAB)
    b16 = table(16 * u, 16 * u, 16 * 128)
    m4, ul = np.meshgrid(np.arange(4), np.arange(32), indexing="ij")
    m4k, ulk = np.meshgrid(np.arange(4), np.arange(64), indexing="ij")
    tk4 = (16 * ulk + 4 * m4k).reshape(-1)
    b4_first = table((16 * ul + 4 * m4).reshape(-1), tk4, 4 * 128)
    b4 = table((16 * (32 + ul) + 4 * m4).reshape(-1), tk4, 4 * 128)
    r1, ul1 = np.meshgrid(np.arange(16), np.arange(D1_Q), indexing="ij")
    r1k, ul1k = np.meshgrid(np.arange(16), np.arange(2 * D1_Q), indexing="ij")
    tk1 = (16 * ul1k + r1k).reshape(-1)
    b1_first = table((16 * ul1 + r1).reshape(-1), tk1, 128)
    b1 = table((16 * (D1_Q + ul1) + r1).reshape(-1), tk1, 128)
    return b16, b4_first, b4, b1_first, b1


def _rows(ref, starts, size):
    return jnp.concatenate([ref[s:s + size, :] for s in starts], axis=0)


def _scatter(ref, starts, size, val):
    for i, st in enumerate(starts):
        ref[st:st + size, :] = val[i * size:(i + 1) * size, :]


def _d4_blocks():
    for r4 in range(4):
        slabs = [(r4 + 4 * m) * SLAB for m in range(4)]
        for n in range(4):
            yield [b + 32 * n for b in slabs], [b + 32 * max(n - 1, 0) for b in slabs], n == 0


D1_Q = 8


def _d1_blocks():
    for n in range(SLAB // D1_Q):
        yield ([r * SLAB + D1_Q * n for r in range(STRIDE)],
               [r * SLAB + D1_Q * max(n - 1, 0) for r in range(STRIDE)], n == 0)


def _attn_body(q_ref, k_ref, v_ref, b16_ref, b4f_ref, b4_ref, b1f_ref, b1_ref, o_ref,
               s16, s4, s1, p16, p4, p1, a16, a4, linv):
    def qk(q, k):
        return lax.dot_general(q, k, (((1,), (1,)), ((), ())), preferred_element_type=F32)

    for r in range(STRIDE):
        rows = slice(r * SLAB, (r + 1) * SLAB)
        s16[rows, :] = qk(q_ref[rows, :], k_ref[rows, :]) + b16_ref[...]
        yield
    for q_st, k_st, first in _d4_blocks():
        s = qk(_rows(q_ref, q_st, 32), _rows(k_ref, k_st, 64)) + (b4f_ref if first else b4_ref)[...]
        _scatter(s4, q_st, 32, s)
        yield
    for q_st, k_st, first in _d1_blocks():
        s = qk(_rows(q_ref, q_st, D1_Q), _rows(k_ref, k_st, 2 * D1_Q)) + (b1f_ref if first else b1_ref)[...]
        _scatter(s1, q_st, D1_Q, s)
        yield

    for r in range(STRIDE):
        rows = slice(r * SLAB, (r + 1) * SLAB)
        x16, x4, x1 = s16[rows, :], s4[rows, :], s1[rows, :]
        m = jnp.maximum(jnp.maximum(jnp.max(x16, axis=-1, keepdims=True), jnp.max(x4, axis=-1, keepdims=True)),
                        jnp.max(x1, axis=-1, keepdims=True))
        e16 = jnp.exp2(x16 - m)
        e4 = jnp.exp2(x4 - m)
        e1 = jnp.exp2(x1 - m)
        l = (jnp.sum(e16, axis=-1, keepdims=True) + jnp.sum(e4, axis=-1, keepdims=True)
             + jnp.sum(e1, axis=-1, keepdims=True))
        p16[rows, :] = e16.astype(BF16)
        p4[rows, :] = e4.astype(BF16)
        p1[rows, :] = e1.astype(BF16)
        linv[rows, :] = jnp.broadcast_to(1.0 / l, (SLAB, HEAD_DIM))
        yield

    for r in range(STRIDE):
        rows = slice(r * SLAB, (r + 1) * SLAB)
        a16[rows, :] = jnp.dot(p16[rows, :], v_ref[rows, :], preferred_element_type=F32)
        yield
    for q_st, k_st, _ in _d4_blocks():
        _scatter(a4, q_st, 32, jnp.dot(_rows(p4, q_st, 32), _rows(v_ref, k_st, 64), preferred_element_type=F32))
        yield
    for q_st, k_st, _ in _d1_blocks():
        acc = jnp.dot(_rows(p1, q_st, D1_Q), _rows(v_ref, k_st, 2 * D1_Q), preferred_element_type=F32)
        acc = acc + _rows(a16, q_st, D1_Q) + _rows(a4, q_st, D1_Q)
        _scatter(o_ref, q_st, D1_Q, (acc * _rows(linv, q_st, D1_Q)).astype(o_ref.dtype))
        yield


def _shift_rows(x, k, fill):
    n = x.shape[0]
    if k % 8 == 0:
        return jnp.concatenate([jnp.full((k, x.shape[1]), fill, x.dtype), x[:n - k, :]], axis=0)
    rolled = pltpu.roll(x, k, axis=0)
    row = lax.broadcasted_iota(jnp.int32, x.shape, 0)
    return jnp.where(row < k, fill, rolled)


def _lru_body(xr_ref, gg_ref, cw_ref, cb_ref, wa_ref, ba_ref, wx_ref, bx_ref, lam_ref,
              o_ref, x32_ref, a_ref, b_ref):
    cw = cw_ref[...]
    cb = cb_ref[...]
    x32_ref[...] = xr_ref[...].astype(F32)

    def tap(r):
        x = x32_ref[(r % STRIDE) * SLAB:(r % STRIDE + 1) * SLAB, :]
        return x if r >= 0 else _shift_rows(x, 1, 0.0)

    w_gates = jnp.concatenate([wa_ref[...], wx_ref[...]], axis=1).astype(BF16)
    z = -lam_ref[...]
    softplus = jnp.maximum(z, 0.0) + jnp.log1p(jnp.exp(-jnp.abs(z)))
    c8 = (-LRU_C) * softplus

    ta = tb = None
    for r in range(STRIDE):
        cur = slice(r * SLAB, (r + 1) * SLAB)
        xc = cb + cw[3:4, :] * tap(r)
        for j in range(1, CONV_W):
            xc = xc + cw[3 - j:4 - j, :] * tap(r - j)
        gates = jnp.dot(xc.astype(BF16), w_gates, preferred_element_type=F32)
        rg = _sigmoid(gates[:, :HEAD_DIM] + ba_ref[...])
        ig = _sigmoid(gates[:, HEAD_DIM:] + bx_ref[...])
        log_a = rg * c8
        a = jnp.exp(log_a)
        m2 = -jnp.tanh(log_a) * (a * a + 1.0)
        mult = jnp.where(m2 == 0.0, 0.0, m2 * lax.rsqrt(m2))
        b = mult * (ig * xc)
        if r > 0:
            b = a * tb + b
            a = a * ta
        a_ref[cur, :] = a
        b_ref[cur, :] = b
        ta, tb = a, b
        yield

    k = 1
    while k < SLAB:
        tb = ta * _shift_rows(tb, k, 0.0) + tb
        ta = ta * _shift_rows(ta, k, 1.0)
        k *= 2
    carry = _shift_rows(tb, 1, 0.0)
    yield

    for r in range(STRIDE):
        cur = slice(r * SLAB, (r + 1) * SLAB)
        h = b_ref[cur, :] + a_ref[cur, :] * carry
        o_ref[cur, :] = (h * gg_ref[cur, :].astype(F32)).astype(o_ref.dtype)
        yield


N_ATTN_IN = 8
N_LRU_IN = 9
N_ATTN_SCRATCH = 9
ATTN_ITEMS = 2 * (STRIDE + 16 + SLAB // D1_Q) + STRIDE
LRU_ITEMS = 2 * STRIDE + 1


def _mixers_kernel(*refs):
    attn_in = refs[:N_ATTN_IN]
    lru_in = refs[N_ATTN_IN:N_ATTN_IN + N_LRU_IN]
    o_attn, o_lru = refs[N_ATTN_IN + N_LRU_IN:N_ATTN_IN + N_LRU_IN + 2]
    scratch = refs[N_ATTN_IN + N_LRU_IN + 2:]
    attn = _attn_body(*attn_in, o_attn, *scratch[:N_ATTN_SCRATCH])
    lru = _lru_body(*lru_in, o_lru, *scratch[N_ATTN_SCRATCH:])
    emitted = 0
    for i in range(ATTN_ITEMS):
        next(attn)
        while emitted * ATTN_ITEMS < (i + 1) * LRU_ITEMS:
            next(lru)
            emitted += 1
    assert next(attn, None) is None and next(lru, None) is None


def _mixers(proj, batch, conv_w, conv_b, wa, ba, wx, bx, lam):
    tables = [jnp.asarray(t) for t in _bias_tables()]
    per_head = lambda t: pl.BlockSpec((None,) + t.shape[1:], lambda h, b: (h, 0, 0))
    col = lambda off: pl.BlockSpec((SEQ, HEAD_DIM), lambda h, b: (b, off + h))
    vec = lambda rows: pl.BlockSpec((rows, HEAD_DIM), lambda h, b: (0, h))
    mat = pl.BlockSpec((None, HEAD_DIM, HEAD_DIM), lambda h, b: (h, 0, 0))
    out = pl.BlockSpec((SEQ, HEAD_DIM), lambda h, b: (b, h))
    f32_rows = lambda w: pltpu.VMEM((SEQ, w), F32)
    bf16_rows = lambda w: pltpu.VMEM((SEQ, w), BF16)
    return pl.pallas_call(
        _mixers_kernel,
        grid=(HEADS, batch),
        in_specs=[col(0), col(HEADS), col(2 * HEADS)] + [per_head(t) for t in tables]
                 + [col(3 * HEADS), col(4 * HEADS), vec(CONV_W), vec(1), mat, vec(1), mat, vec(1), vec(1)],
        out_specs=[out, out],
        out_shape=[jax.ShapeDtypeStruct((batch * SEQ, D_MODEL), BF16)] * 2,
        scratch_shapes=[f32_rows(128), f32_rows(256), f32_rows(256), bf16_rows(128), bf16_rows(256), bf16_rows(256),
                        f32_rows(HEAD_DIM), f32_rows(HEAD_DIM), f32_rows(HEAD_DIM)]
                       + [f32_rows(HEAD_DIM)] * 3,
        compiler_params=_cparams("parallel", "parallel"),
        name="token_mixers",
    )(proj, proj, proj, *tables, proj, proj, conv_w, conv_b.reshape(1, -1), wa, ba.reshape(1, -1), wx,
      bx.reshape(1, -1), lam.reshape(1, -1))


MG_TM = 1024
MG_TN = 1024


def _merge_kernel(ya_ref, yl_ref, ga_ref, gl_ref, wa_ref, wl_ref, o_ref):
    pa = jnp.dot(ya_ref[...], wa_ref[...], preferred_element_type=F32)
    pl_ = jnp.dot(yl_ref[...], wl_ref[...], preferred_element_type=F32)
    merged = _sigmoid(ga_ref[...].astype(F32)) * pa + _sigmoid(gl_ref[...].astype(F32)) * pl_
    o_ref[...] = merged.astype(o_ref.dtype)


def _merge(y_attn, y_lru, proj, w_pa, w_pl):
    t = y_attn.shape[0]
    nj = D_MODEL // MG_TN
    act = pl.BlockSpec((MG_TM, D_MODEL), lambda i, j: (i, 0))
    gate = lambda off: pl.BlockSpec((MG_TM, MG_TN), lambda i, j: (i, off * nj + j))
    w = pl.BlockSpec((D_MODEL, MG_TN), lambda i, j: (0, j))
    return pl.pallas_call(
        _merge_kernel,
        grid=(t // MG_TM, nj),
        in_specs=[act, act, gate(5), gate(6), w, w],
        out_specs=pl.BlockSpec((MG_TM, MG_TN), lambda i, j: (i, j)),
        out_shape=jax.ShapeDtypeStruct((t, D_MODEL), BF16),
        compiler_params=_cparams("parallel", "arbitrary"),
        name="gated_merge",
    )(y_attn, y_lru, proj, proj, w_pa, w_pl)


OUT_TN = 512


def _out_proj_kernel(x_hbm, m_ref, w_ref, h_hbm, x_buf, h_buf, x_sem, h_sem):
    b = pl.program_id(0)
    j = pl.program_id(1)
    nb = pl.num_programs(0)
    nj = pl.num_programs(1)
    step = b * nj + j
    slot = lax.rem(step, 2)

    def gather(r):
        cols = pl.ds(pl.multiple_of(j * OUT_TN, OUT_TN), OUT_TN)
        return pltpu.make_async_copy(x_hbm.at[b, :, r, cols], x_buf.at[r], x_sem)

    def scatter(bi, ji, r, sl):
        cols = pl.ds(pl.multiple_of(ji * OUT_TN, OUT_TN), OUT_TN)
        return pltpu.make_async_copy(h_buf.at[sl, r], h_hbm.at[bi, :, r, cols], h_sem.at[sl])

    for r in range(STRIDE):
        gather(r).start()
    w = w_ref[:, pl.ds(pl.multiple_of(j * OUT_TN, OUT_TN), OUT_TN)]
    y = jnp.dot(m_ref[...], w, preferred_element_type=F32)
    for r in range(STRIDE):
        gather(r).wait()
    for r in range(STRIDE):
        h_buf[slot, r] = x_buf[r] + y[r * SLAB:(r + 1) * SLAB, :]

    @pl.when(step > 0)
    def _():
        prev_b = jnp.where(j == 0, b - 1, b)
        prev_j = jnp.where(j == 0, nj - 1, j - 1)
        for r in range(STRIDE):
            scatter(prev_b, prev_j, r, 1 - slot).wait()

    for r in range(STRIDE):
        scatter(b, j, r, slot).start()

    @pl.when(step == nb * nj - 1)
    def _():
        for r in range(STRIDE):
            scatter(b, j, r, slot).wait()


def _out_proj(x4, merged, w_out):
    b = x4.shape[0]
    return pl.pallas_call(
        _out_proj_kernel,
        grid=(b, D_MODEL // OUT_TN),
        in_specs=[
            pl.BlockSpec(memory_space=pl.ANY),
            pl.BlockSpec((SEQ, D_MODEL), lambda bi, j: (bi, 0)),
            pl.BlockSpec((D_MODEL, D_MODEL), lambda bi, j: (0, 0)),
        ],
        out_specs=pl.BlockSpec(memory_space=pl.ANY),
        out_shape=jax.ShapeDtypeStruct(x4.shape, F32),
        scratch_shapes=[pltpu.VMEM((STRIDE, SLAB, OUT_TN), F32), pltpu.VMEM((2, STRIDE, SLAB, OUT_TN), F32),
                        pltpu.SemaphoreType.DMA(()), pltpu.SemaphoreType.DMA((2,))],
        compiler_params=_cparams("arbitrary", "arbitrary"),
        name="out_proj_residual",
    )(x4, merged, w_out)


MLP_TM = 1024
MLP_TF = 1024
MLP_VMEM_LIMIT = V7X_VMEM_BYTES - 2 * 1024 * 1024


def _mlp_kernel(h_ref, g_ref, wu_ref, wd_ref, gf_ref, o_ref, hn_ref):
    f = pl.program_id(1)

    @pl.when(f == 0)
    def _():
        h = h_ref[...]
        ms = jnp.mean(h * h, axis=-1, keepdims=True)
        hn_ref[...] = (h * lax.rsqrt(ms + EPS) * g_ref[...]).astype(BF16)

    def accumulate(first):
        up = jnp.dot(hn_ref[...], wu_ref[...], preferred_element_type=F32)
        hid = jnp.square(jnp.maximum(up, 0.0)).astype(BF16)
        part = jnp.dot(hid, wd_ref[...], preferred_element_type=F32)
        if first:
            o_ref[...] = part
        else:
            o_ref[...] += part

    pl.when(f == 0)(lambda: accumulate(True))
    pl.when(f > 0)(lambda: accumulate(False))

    @pl.when(f == pl.num_programs(1) - 1)
    def _():
        h2 = h_ref[...] + o_ref[...]
        ms = jnp.mean(h2 * h2, axis=-1, keepdims=True)
        o_ref[...] = h2 * lax.rsqrt(ms + EPS) * gf_ref[...]


def _mlp(h1, g_mlp, w_up, w_down, g_final):
    t = h1.shape[0]
    return pl.pallas_call(
        _mlp_kernel,
        grid=(t // MLP_TM, D_FF // MLP_TF),
        in_specs=[
            pl.BlockSpec((MLP_TM, D_MODEL), lambda i, f: (i, 0)),
            pl.BlockSpec((1, D_MODEL), lambda i, f: (0, 0)),
            pl.BlockSpec((D_MODEL, MLP_TF), lambda i, f: (0, f)),
            pl.BlockSpec((MLP_TF, D_MODEL), lambda i, f: (f, 0)),
            pl.BlockSpec((1, D_MODEL), lambda i, f: (0, 0)),
        ],
        out_specs=pl.BlockSpec((MLP_TM, D_MODEL), lambda i, f: (i, 0)),
        out_shape=jax.ShapeDtypeStruct((t, D_MODEL), F32),
        scratch_shapes=[pltpu.VMEM((MLP_TM, D_MODEL), BF16)],
        compiler_params=pltpu.CompilerParams(dimension_semantics=("parallel", "arbitrary"),
                                             vmem_limit_bytes=MLP_VMEM_LIMIT),
        name="mlp_final_norm",
    )(h1, g_mlp.reshape(1, -1), w_up, w_down, g_final.reshape(1, -1))


def kernel(x, norm_mix_g, w_in, conv_w, conv_b, lru_wa, lru_ba, lru_wx, lru_bx, lru_lambda,
           w_proj_attn, w_proj_lru, w_out, norm_mlp_g, w_up, w_down, norm_final_g):
    assert w_in.shape[0] == 1, "single-layer block"
    batch = x.shape[0]
    x4 = x.reshape(batch, SLAB, STRIDE, D_MODEL)
    proj, (w_pa, w_pl, w_o, w_u, w_d) = _in_proj(
        _norm_permute(x4, norm_mix_g[0]), w_in[0],
        [w_proj_attn[0], w_proj_lru[0], w_out[0], w_up[0], w_down[0]])
    y_attn, y_lru = _mixers(proj, batch, conv_w[0], conv_b[0], lru_wa[0], lru_ba[0], lru_wx[0], lru_bx[0],
                            lru_lambda[0])
    merged = _merge(y_attn, y_lru, proj, w_pa, w_pl)
    h1 = _out_proj(x4, merged, w_o)
    out = _mlp(h1.reshape(batch * SEQ, D_MODEL), norm_mlp_g[0], w_u, w_d, norm_final_g)
    return out.reshape(batch, SEQ, D_MODEL)
```

```python
import jax
import jax.numpy as jnp
import numpy as np
from jax import lax
from jax.experimental import pallas as pl
from jax.experimental.pallas import tpu as pltpu

D_MODEL = 2048
SEQ = 2048
HEADS = 16
HEAD_DIM = 128
D_FF = 4 * D_MODEL
IN_COLS = 7 * D_MODEL
CONV_W = 4
LRU_C = 8.0
EPS = 1e-6

STRIDE = 16
SLAB = SEQ // STRIDE
HALF = STRIDE // 2
MASKED = -1e30
LOG2E = 1.4426950408889634
Q_SCALE = HEAD_DIM ** -0.5 * LOG2E

V7X_VMEM_BYTES = 64 * 1024 * 1024
VMEM_LIMIT = V7X_VMEM_BYTES - 8 * 1024 * 1024

BF16 = jnp.bfloat16
F32 = jnp.float32


def _cparams(*sem):
    return pltpu.CompilerParams(dimension_semantics=sem, vmem_limit_bytes=VMEM_LIMIT)


def _sigmoid(x):
    return 1.0 / (1.0 + jnp.exp2(x * (-LOG2E)))


def _gelu_tanh(x):
    c = 0.7978845608028654
    return x * (0.5 + 0.5 * jnp.tanh(x * (c + (c * 0.044715) * (x * x))))


NORM_TM = HALF * SLAB
IN_TM = SEQ
IN_TN = 1024
LRU_GATE_SPLIT = 4
CAST_STEPS = 8
IN_VMEM_LIMIT = V7X_VMEM_BYTES - 4 * 1024 * 1024


def _norm_kernel(x_hbm, g_ref, o_ref, x_buf, x_sem):
    b = pl.program_id(0)
    i = pl.program_id(1)
    step = b * 2 + i
    last = pl.num_programs(0) * 2 - 1
    slot = lax.rem(step, 2)

    def gather(bi, half, sl, s):
        return pltpu.make_async_copy(x_hbm.at[bi, :, half * HALF + s, :], x_buf.at[sl, s], x_sem.at[sl])

    @pl.when(step == 0)
    def _():
        for s in range(HALF):
            gather(b, i, slot, s).start()

    @pl.when(step < last)
    def _():
        for s in range(HALF):
            gather(b + i, 1 - i, 1 - slot, s).start()

    for s in range(HALF):
        gather(b, i, slot, s).wait()
    for s in range(HALF):
        xs = x_buf[slot, s]
        ms = jnp.mean(xs * xs, axis=-1, keepdims=True)
        o_ref[s * SLAB:(s + 1) * SLAB, :] = (xs * lax.rsqrt(ms + EPS) * g_ref[...]).astype(o_ref.dtype)


def _norm_permute(x4, g):
    b = x4.shape[0]
    return pl.pallas_call(
        _norm_kernel,
        grid=(b, 2),
        in_specs=[
            pl.BlockSpec(memory_space=pl.ANY),
            pl.BlockSpec((1, D_MODEL), lambda bi, i: (0, 0)),
        ],
        out_specs=pl.BlockSpec((NORM_TM, D_MODEL), lambda bi, i: (bi * 2 + i, 0)),
        out_shape=jax.ShapeDtypeStruct((b * SEQ, D_MODEL), BF16),
        scratch_shapes=[pltpu.VMEM((2, HALF, SLAB, D_MODEL), F32), pltpu.SemaphoreType.DMA((2,))],
        compiler_params=_cparams("arbitrary", "arbitrary"),
        name="norm_permute",
    )(x4, g.reshape(1, D_MODEL))


def _in_proj_kernel(xn_ref, w_ref, *refs):
    n_side = (len(refs) - 1) // 2
    side_in, o_ref, side_out = refs[:n_side], refs[n_side], refs[n_side + 1:]
    j = pl.program_id(1)
    tiles = D_MODEL // IN_TN
    is_q = j < tiles
    is_lru_gate = jnp.logical_and(j >= LRU_GATE_SPLIT * tiles, j < (LRU_GATE_SPLIT + 1) * tiles)

    def project(epilogue):
        y = jnp.dot(xn_ref[...], w_ref[...].astype(BF16), preferred_element_type=F32)
        o_ref[...] = epilogue(y).astype(o_ref.dtype)
        for src, dst in zip(side_in, side_out):
            dst[...] = src[...].astype(dst.dtype)

    pl.when(is_q)(lambda: project(lambda y: y * Q_SCALE))
    pl.when(is_lru_gate)(lambda: project(_gelu_tanh))
    pl.when(jnp.logical_not(jnp.logical_or(is_q, is_lru_gate)))(lambda: project(lambda y: y))


def _in_proj(xn, w, later_weights):
    t = xn.shape[0]
    n_i = t // IN_TM
    slices = n_i * CAST_STEPS

    def side_spec(wt):
        rows, rem = divmod(wt.shape[0], slices)
        assert rem == 0 and rows % 16 == 0, wt.shape
        return pl.BlockSpec((rows, wt.shape[1]), lambda i, j: (i * CAST_STEPS + jnp.minimum(j, CAST_STEPS - 1), 0))

    side_specs = [side_spec(wt) for wt in later_weights]
    outs = pl.pallas_call(
        _in_proj_kernel,
        grid=(n_i, IN_COLS // IN_TN),
        in_specs=[
            pl.BlockSpec((IN_TM, D_MODEL), lambda i, j: (i, 0)),
            pl.BlockSpec((D_MODEL, IN_TN), lambda i, j: (0, j)),
        ] + side_specs,
        out_specs=[pl.BlockSpec((IN_TM, IN_TN), lambda i, j: (i, j))] + side_specs,
        out_shape=[jax.ShapeDtypeStruct((t, IN_COLS), BF16)]
                  + [jax.ShapeDtypeStruct(wt.shape, BF16) for wt in later_weights],
        compiler_params=pltpu.CompilerParams(dimension_semantics=("parallel", "arbitrary"),
                                             vmem_limit_bytes=IN_VMEM_LIMIT),
        name="in_proj",
    )(xn, w, *later_weights)
    return outs[0], outs[1:]


def _bias_tables():
    slopes = 2.0 ** (-8.0 * np.arange(1, HEADS + 1, dtype=np.float64) / HEADS)

    def table(tq, tk, span):
        dist = (tq[:, None] - tk[None, :]).astype(np.float64)
        ok = (dist >= 0) & (dist <= span)
        bias = -slopes[:, None, None] * dist[None] * LOG2E
        return np.where(ok[None], bias, MASKED).astype(np.float32)

    u = np.arange(SLAB)
    b16 = table(16 * u, 16 * u, 16 * 128)
    m4, ul = np.meshgrid(np.arange(4), np.arange(32), indexing="ij")
    m4k, ulk = np.meshgrid(np.arange(4), np.arange(64), indexing="ij")
    tk4 = (16 * ulk + 4 * m4k).reshape(-1)
    b4_first = table((16 * ul + 4 * m4).reshape(-1), tk4, 4 * 128)
    b4 = table((16 * (32 + ul) + 4 * m4).reshape(-1), tk4, 4 * 128)
    r1, ul1 = np.meshgrid(np.arange(16), np.arange(D1_Q), indexing="ij")
    r1k, ul1k = np.meshgrid(np.arange(16), np.arange(2 * D1_Q), indexing="ij")
    tk1 = (16 * ul1k + r1k).reshape(-1)
    b1_first = table((16 * ul1 + r1).reshape(-1), tk1, 128)
    b1 = table((16 * (D1_Q + ul1) + r1).reshape(-1), tk1, 128)
    return b16, b4_first, b4, b1_first, b1


def _rows(ref, starts, size):
    return jnp.concatenate([ref[s:s + size, :] for s in starts], axis=0)


def _scatter(ref, starts, size, val):
    for i, st in enumerate(starts):
        ref[st:st + size, :] = val[i * size:(i + 1) * size, :]


def _d4_blocks():
    for r4 in range(4):
        slabs = [(r4 + 4 * m) * SLAB for m in range(4)]
        for n in range(4):
            yield [b + 32 * n for b in slabs], [b + 32 * max(n - 1, 0) for b in slabs], n == 0


D1_Q = 8


def _d1_blocks():
    for n in range(SLAB // D1_Q):
        yield ([r * SLAB + D1_Q * n for r in range(STRIDE)],
               [r * SLAB + D1_Q * max(n - 1, 0) for r in range(STRIDE)], n == 0)


def _attn_body(q_ref, k_ref, v_ref, b16_ref, b4f_ref, b4_ref, b1f_ref, b1_ref, o_ref,
               s16, s4, s1, p16, p4, p1, a16, a4, linv):
    def qk(q, k):
        return lax.dot_general(q, k, (((1,), (1,)), ((), ())), preferred_element_type=F32)

    for r in range(STRIDE):
        rows = slice(r * SLAB, (r + 1) * SLAB)
        s16[rows, :] = qk(q_ref[rows, :], k_ref[rows, :]) + b16_ref[...]
        yield
    for q_st, k_st, first in _d4_blocks():
        s = qk(_rows(q_ref, q_st, 32), _rows(k_ref, k_st, 64)) + (b4f_ref if first else b4_ref)[...]
        _scatter(s4, q_st, 32, s)
        yield
    for q_st, k_st, first in _d1_blocks():
        s = qk(_rows(q_ref, q_st, D1_Q), _rows(k_ref, k_st, 2 * D1_Q)) + (b1f_ref if first else b1_ref)[...]
        _scatter(s1, q_st, D1_Q, s)
        yield

    for r in range(STRIDE):
        rows = slice(r * SLAB, (r + 1) * SLAB)
        x16, x4, x1 = s16[rows, :], s4[rows, :], s1[rows, :]
        m = jnp.maximum(jnp.maximum(jnp.max(x16, axis=-1, keepdims=True), jnp.max(x4, axis=-1, keepdims=True)),
                        jnp.max(x1, axis=-1, keepdims=True))
        e16 = jnp.exp2(x16 - m)
        e4 = jnp.exp2(x4 - m)
        e1 = jnp.exp2(x1 - m)
        l = (jnp.sum(e16, axis=-1, keepdims=True) + jnp.sum(e4, axis=-1, keepdims=True)
             + jnp.sum(e1, axis=-1, keepdims=True))
        p16[rows, :] = e16.astype(BF16)
        p4[rows, :] = e4.astype(BF16)
        p1[rows, :] = e1.astype(BF16)
        linv[rows, :] = jnp.broadcast_to(1.0 / l, (SLAB, HEAD_DIM))
        yield

    for r in range(STRIDE):
        rows = slice(r * SLAB, (r + 1) * SLAB)
        a16[rows, :] = jnp.dot(p16[rows, :], v_ref[rows, :], preferred_element_type=F32)
        yield
    for q_st, k_st, _ in _d4_blocks():
        _scatter(a4, q_st, 32, jnp.dot(_rows(p4, q_st, 32), _rows(v_ref, k_st, 64), preferred_element_type=F32))
        yield
    for q_st, k_st, _ in _d1_blocks():
        acc = jnp.dot(_rows(p1, q_st, D1_Q), _rows(v_ref, k_st, 2 * D1_Q), preferred_element_type=F32)
        acc = acc + _rows(a16, q_st, D1_Q) + _rows(a4, q_st, D1_Q)
        _scatter(o_ref, q_st, D1_Q, (acc * _rows(linv, q_st, D1_Q)).astype(o_ref.dtype))
        yield


def _shift_rows(x, k, fill):
    n = x.shape[0]
    if k % 8 == 0:
        return jnp.concatenate([jnp.full((k, x.shape[1]), fill, x.dtype), x[:n - k, :]], axis=0)
    rolled = pltpu.roll(x, k, axis=0)
    row = lax.broadcasted_iota(jnp.int32, x.shape, 0)
    return jnp.where(row < k, fill, rolled)


def _lru_body(xr_ref, gg_ref, cw_ref, cb_ref, wa_ref, ba_ref, wx_ref, bx_ref, lam_ref,
              o_ref, x32_ref, a_ref, b_ref):
    cw = cw_ref[...]
    cb = cb_ref[...]
    x32_ref[...] = xr_ref[...].astype(F32)

    def tap(r):
        x = x32_ref[(r % STRIDE) * SLAB:(r % STRIDE + 1) * SLAB, :]
        return x if r >= 0 else _shift_rows(x, 1, 0.0)

    w_gates = jnp.concatenate([wa_ref[...], wx_ref[...]], axis=1).astype(BF16)
    z = -lam_ref[...]
    softplus = jnp.maximum(z, 0.0) + jnp.log1p(jnp.exp(-jnp.abs(z)))
    c8 = (-LRU_C) * softplus

    ta = tb = None
    for r in range(STRIDE):
        cur = slice(r * SLAB, (r + 1) * SLAB)
        xc = cb + cw[3:4, :] * tap(r)
        for j in range(1, CONV_W):
            xc = xc + cw[3 - j:4 - j, :] * tap(r - j)
        gates = jnp.dot(xc.astype(BF16), w_gates, preferred_element_type=F32)
        rg = _sigmoid(gates[:, :HEAD_DIM] + ba_ref[...])
        ig = _sigmoid(gates[:, HEAD_DIM:] + bx_ref[...])
        log_a = rg * c8
        a = jnp.exp(log_a)
        m2 = -jnp.tanh(log_a) * (a * a + 1.0)
        mult = jnp.where(m2 == 0.0, 0.0, m2 * lax.rsqrt(m2))
        b = mult * (ig * xc)
        if r > 0:
            b = a * tb + b
            a = a * ta
        a_ref[cur, :] = a
        b_ref[cur, :] = b
        ta, tb = a, b
        yield

    k = 1
    while k < SLAB:
        tb = ta * _shift_rows(tb, k, 0.0) + tb
        ta = ta * _shift_rows(ta, k, 1.0)
        k *= 2
    carry = _shift_rows(tb, 1, 0.0)
    yield

    for r in range(STRIDE):
        cur = slice(r * SLAB, (r + 1) * SLAB)
        h = b_ref[cur, :] + a_ref[cur, :] * carry
        o_ref[cur, :] = (h * gg_ref[cur, :].astype(F32)).astype(o_ref.dtype)
        yield


N_ATTN_IN = 8
N_LRU_IN = 9
N_ATTN_SCRATCH = 9
ATTN_ITEMS = 2 * (STRIDE + 16 + SLAB // D1_Q) + STRIDE
LRU_ITEMS = 2 * STRIDE + 1


def _mixers_kernel(*refs):
    attn_in = refs[:N_ATTN_IN]
    lru_in = refs[N_ATTN_IN:N_ATTN_IN + N_LRU_IN]
    o_attn, o_lru = refs[N_ATTN_IN + N_LRU_IN:N_ATTN_IN + N_LRU_IN + 2]
    scratch = refs[N_ATTN_IN + N_LRU_IN + 2:]
    attn = _attn_body(*attn_in, o_attn, *scratch[:N_ATTN_SCRATCH])
    lru = _lru_body(*lru_in, o_lru, *scratch[N_ATTN_SCRATCH:])
    emitted = 0
    for i in range(ATTN_ITEMS):
        next(attn)
        while emitted * ATTN_ITEMS < (i + 1) * LRU_ITEMS:
            next(lru)
            emitted += 1
    assert next(attn, None) is None and next(lru, None) is None


def _mixers(proj, batch, conv_w, conv_b, wa, ba, wx, bx, lam):
    tables = [jnp.asarray(t) for t in _bias_tables()]
    per_head = lambda t: pl.BlockSpec((None,) + t.shape[1:], lambda h, b: (h, 0, 0))
    col = lambda off: pl.BlockSpec((SEQ, HEAD_DIM), lambda h, b: (b, off + h))
    vec = lambda rows: pl.BlockSpec((rows, HEAD_DIM), lambda h, b: (0, h))
    mat = pl.BlockSpec((None, HEAD_DIM, HEAD_DIM), lambda h, b: (h, 0, 0))
    out = pl.BlockSpec((SEQ, HEAD_DIM), lambda h, b: (b, h))
    f32_rows = lambda w: pltpu.VMEM((SEQ, w), F32)
    bf16_rows = lambda w: pltpu.VMEM((SEQ, w), BF16)
    return pl.pallas_call(
        _mixers_kernel,
        grid=(HEADS, batch),
        in_specs=[col(0), col(HEADS), col(2 * HEADS)] + [per_head(t) for t in tables]
                 + [col(3 * HEADS), col(4 * HEADS), vec(CONV_W), vec(1), mat, vec(1), mat, vec(1), vec(1)],
        out_specs=[out, out],
        out_shape=[jax.ShapeDtypeStruct((batch * SEQ, D_MODEL), BF16)] * 2,
        scratch_shapes=[f32_rows(128), f32_rows(256), f32_rows(256), bf16_rows(128), bf16_rows(256), bf16_rows(256),
                        f32_rows(HEAD_DIM), f32_rows(HEAD_DIM), f32_rows(HEAD_DIM)]
                       + [f32_rows(HEAD_DIM)] * 3,
        compiler_params=_cparams("parallel", "parallel"),
        name="token_mixers",
    )(proj, proj, proj, *tables, proj, proj, conv_w, conv_b.reshape(1, -1), wa, ba.reshape(1, -1), wx,
      bx.reshape(1, -1), lam.reshape(1, -1))


MG_TM = 1024
MG_TN = 1024


def _merge_kernel(ya_ref, yl_ref, ga_ref, gl_ref, wa_ref, wl_ref, o_ref):
    pa = jnp.dot(ya_ref[...], wa_ref[...], preferred_element_type=F32)
    pl_ = jnp.dot(yl_ref[...], wl_ref[...], preferred_element_type=F32)
    merged = _sigmoid(ga_ref[...].astype(F32)) * pa + _sigmoid(gl_ref[...].astype(F32)) * pl_
    o_ref[...] = merged.astype(o_ref.dtype)


def _merge(y_attn, y_lru, proj, w_pa, w_pl):
    t = y_attn.shape[0]
    nj = D_MODEL // MG_TN
    act = pl.BlockSpec((MG_TM, D_MODEL), lambda i, j: (i, 0))
    gate = lambda off: pl.BlockSpec((MG_TM, MG_TN), lambda i, j: (i, off * nj + j))
    w = pl.BlockSpec((D_MODEL, MG_TN), lambda i, j: (0, j))
    return pl.pallas_call(
        _merge_kernel,
        grid=(t // MG_TM, nj),
        in_specs=[act, act, gate(5), gate(6), w, w],
        out_specs=pl.BlockSpec((MG_TM, MG_TN), lambda i, j: (i, j)),
        out_shape=jax.ShapeDtypeStruct((t, D_MODEL), BF16),
        compiler_params=_cparams("parallel", "arbitrary"),
        name="gated_merge",
    )(y_attn, y_lru, proj, proj, w_pa, w_pl)


OUT_TN = 512


def _out_proj_kernel(x_hbm, m_ref, w_ref, h_hbm, x_buf, h_buf, x_sem, h_sem):
    b = pl.program_id(0)
    j = pl.program_id(1)
    nb = pl.num_programs(0)
    nj = pl.num_programs(1)
    step = b * nj + j
    slot = lax.rem(step, 2)

    def gather(r):
        cols = pl.ds(pl.multiple_of(j * OUT_TN, OUT_TN), OUT_TN)
        return pltpu.make_async_copy(x_hbm.at[b, :, r, cols], x_buf.at[r], x_sem)

    def scatter(bi, ji, r, sl):
        cols = pl.ds(pl.multiple_of(ji * OUT_TN, OUT_TN), OUT_TN)
        return pltpu.make_async_copy(h_buf.at[sl, r], h_hbm.at[bi, :, r, cols], h_sem.at[sl])

    for r in range(STRIDE):
        gather(r).start()
    w = w_ref[:, pl.ds(pl.multiple_of(j * OUT_TN, OUT_TN), OUT_TN)]
    y = jnp.dot(m_ref[...], w, preferred_element_type=F32)
    for r in range(STRIDE):
        gather(r).wait()
    for r in range(STRIDE):
        h_buf[slot, r] = x_buf[r] + y[r * SLAB:(r + 1) * SLAB, :]

    @pl.when(step > 0)
    def _():
        prev_b = jnp.where(j == 0, b - 1, b)
        prev_j = jnp.where(j == 0, nj - 1, j - 1)
        for r in range(STRIDE):
            scatter(prev_b, prev_j, r, 1 - slot).wait()

    for r in range(STRIDE):
        scatter(b, j, r, slot).start()

    @pl.when(step == nb * nj - 1)
    def _():
        for r in range(STRIDE):
            scatter(b, j, r, slot).wait()


def _out_proj(x4, merged, w_out):
    b = x4.shape[0]
    return pl.pallas_call(
        _out_proj_kernel,
        grid=(b, D_MODEL // OUT_TN),
        in_specs=[
            pl.BlockSpec(memory_space=pl.ANY),
            pl.BlockSpec((SEQ, D_MODEL), lambda bi, j: (bi, 0)),
            pl.BlockSpec((D_MODEL, D_MODEL), lambda bi, j: (0, 0)),
        ],
        out_specs=pl.BlockSpec(memory_space=pl.ANY),
        out_shape=jax.ShapeDtypeStruct(x4.shape, F32),
        scratch_shapes=[pltpu.VMEM((STRIDE, SLAB, OUT_TN), F32), pltpu.VMEM((2, STRIDE, SLAB, OUT_TN), F32),
                        pltpu.SemaphoreType.DMA(()), pltpu.SemaphoreType.DMA((2,))],
        compiler_params=_cparams("arbitrary", "arbitrary"),
        name="out_proj_residual",
    )(x4, merged, w_out)


MLP_TM = 1024
MLP_TF = 1024
MLP_VMEM_LIMIT = V7X_VMEM_BYTES - 2 * 1024 * 1024


def _mlp_kernel(h_ref, g_ref, wu_ref, wd_ref, gf_ref, o_ref, hn_ref):
    f = pl.program_id(1)

    @pl.when(f == 0)
    def _():
        h = h_ref[...]
        ms = jnp.mean(h * h, axis=-1, keepdims=True)
        hn_ref[...] = (h * lax.rsqrt(ms + EPS) * g_ref[...]).astype(BF16)

    def accumulate(first, last):
        up = jnp.dot(hn_ref[...], wu_ref[...], preferred_element_type=F32)
        hid = jnp.square(jnp.maximum(up, 0.0)).astype(BF16)
        part = jnp.dot(hid, wd_ref[...], preferred_element_type=F32)
        total = (h_ref[...] if first else o_ref[...]) + part
        if last:
            ms = jnp.mean(total * total, axis=-1, keepdims=True)
            total = total * lax.rsqrt(ms + EPS) * gf_ref[...]
        o_ref[...] = total

    n_f = pl.num_programs(1)
    pl.when(f == 0)(lambda: accumulate(True, False))
    pl.when(jnp.logical_and(f > 0, f < n_f - 1))(lambda: accumulate(False, False))
    pl.when(f == n_f - 1)(lambda: accumulate(False, True))


def _mlp(h1, g_mlp, w_up, w_down, g_final):
    t = h1.shape[0]
    assert D_FF // MLP_TF >= 2, "the kernel's first and last f steps must be different steps"
    return pl.pallas_call(
        _mlp_kernel,
        grid=(t // MLP_TM, D_FF // MLP_TF),
        in_specs=[
            pl.BlockSpec((MLP_TM, D_MODEL), lambda i, f: (i, 0)),
            pl.BlockSpec((1, D_MODEL), lambda i, f: (0, 0)),
            pl.BlockSpec((D_MODEL, MLP_TF), lambda i, f: (0, f)),
            pl.BlockSpec((MLP_TF, D_MODEL), lambda i, f: (f, 0)),
            pl.BlockSpec((1, D_MODEL), lambda i, f: (0, 0)),
        ],
        out_specs=pl.BlockSpec((MLP_TM, D_MODEL), lambda i, f: (i, 0)),
        out_shape=jax.ShapeDtypeStruct((t, D_MODEL), F32),
        scratch_shapes=[pltpu.VMEM((MLP_TM, D_MODEL), BF16)],
        compiler_params=pltpu.CompilerParams(dimension_semantics=("parallel", "arbitrary"),
                                             vmem_limit_bytes=MLP_VMEM_LIMIT),
        name="mlp_final_norm",
    )(h1, g_mlp.reshape(1, -1), w_up, w_down, g_final.reshape(1, -1))


def kernel(x, norm_mix_g, w_in, conv_w, conv_b, lru_wa, lru_ba, lru_wx, lru_bx, lru_lambda,
           w_proj_attn, w_proj_lru, w_out, norm_mlp_g, w_up, w_down, norm_final_g):
    assert w_in.shape[0] == 1, "single-layer block"
    batch = x.shape[0]
    x4 = x.reshape(batch, SLAB, STRIDE, D_MODEL)
    proj, (w_pa, w_pl, w_o, w_u, w_d) = _in_proj(
        _norm_permute(x4, norm_mix_g[0]), w_in[0],
        [w_proj_attn[0], w_proj_lru[0], w_out[0], w_up[0], w_down[0]])
    y_attn, y_lru = _mixers(proj, batch, conv_w[0], conv_b[0], lru_wa[0], lru_ba[0], lru_wx[0], lru_bx[0],
                            lru_lambda[0])
    merged = _merge(y_attn, y_lru, proj, w_pa, w_pl)
    h1 = _out_proj(x4, merged, w_o)
    out = _mlp(h1.reshape(batch * SEQ, D_MODEL), norm_mlp_g[0], w_u, w_d, norm_final_g)
    return out.reshape(batch, SEQ, D_MODEL)
```
